```python
import math
import jax, jax.numpy as jnp
from jax import lax
import numpy as np

D_MODEL = 2048
BATCH = 1
SEQ = 16384
DEPTH = 2
DEC_BATCH = 8
DEC_SEQ = 2048
PAST_LEN = 128

N_EVEN = (DEPTH + 1) // 2
N_ODD = DEPTH // 2
RMS_EPS = 1e-6
Q_BLOCK = 128
ROPE_THETA = 10000.0
GRID_W = 64

POOL_WINDOWS = (2, 4, 8, 16)
POOL_GROUP = D_MODEL // 8
POOL_WIDTH = POOL_GROUP * len(POOL_WINDOWS)
MLA_HEADS = 8
MLA_Q_LORA = D_MODEL // 4
MLA_KV_LORA = D_MODEL // 8
MLA_NOPE = 128
MLA_ROPE = 64
MLA_V = 128
IN0_WIDTH = POOL_WIDTH + MLA_Q_LORA + MLA_KV_LORA + MLA_ROPE
MIX0_WIDTH = POOL_WIDTH + MLA_HEADS * MLA_V
DIFF_HEADS = 8
DIFF_HD = 64
GQA_HEADS = 8
GQA_KV = 2
GQA_GROUP = GQA_HEADS // GQA_KV
GQA_HD = 128
AXIAL_DIM = GQA_HD // 2
IN1_WIDTH = 3 * DIFF_HEADS * 2 * DIFF_HD + (GQA_HEADS + 2 * GQA_KV) * GQA_HD
MIX1_WIDTH = DIFF_HEADS * 2 * DIFF_HD + GQA_HEADS * GQA_HD
REL_BUCKETS = 32
REL_MAX_DIST = 128
FFN_DIM = ((8 * D_MODEL // 3 + 255) // 256) * 256
N_EXPERTS = 8
TOP_K = 2
EXPERT_DIM = 7 * D_MODEL // 2

kernel_name = 'hybrid_pool_mla_diff_axialgqa_moe_encoder'


def _rms(x, g):
    xf = x.astype(jnp.float32)
    y = xf * lax.rsqrt(jnp.mean(xf * xf, axis=-1, keepdims=True) + RMS_EPS)
    return (y * g.astype(jnp.float32)).astype(x.dtype)


def _rope_angles(pos, dim):
    inv = ROPE_THETA ** (-jnp.arange(0, dim, 2, dtype=jnp.float32) / dim)
    ang = pos.astype(jnp.float32)[:, None] * inv[None, :]
    return jnp.cos(ang), jnp.sin(ang)


def _apply_rope(x, cos, sin):
    xf = x.astype(jnp.float32)
    x1, x2 = jnp.split(xf, 2, axis=-1)
    c = cos[:, None, :]
    s = sin[:, None, :]
    return jnp.concatenate([x1 * c - x2 * s, x1 * s + x2 * c], axis=-1).astype(x.dtype)


def _sweep(block_fn, seq_len):
    starts = jnp.arange(seq_len // Q_BLOCK, dtype=jnp.int32) * Q_BLOCK
    out = lax.map(block_fn, starts)
    out = jnp.moveaxis(out, 0, 1)
    return out.reshape((out.shape[0], seq_len) + out.shape[3:])


def _t5_bucket(rel):
    nb = REL_BUCKETS // 2
    max_exact = nb // 2
    ret = jnp.where(rel > 0, nb, 0)
    n = jnp.abs(rel)
    n_f = jnp.maximum(n, 1).astype(jnp.float32)
    large = max_exact + (jnp.log(n_f / max_exact) / math.log(REL_MAX_DIST / max_exact)
                         * (nb - max_exact)).astype(jnp.int32)
    large = jnp.minimum(large, nb - 1)
    return ret + jnp.where(n < max_exact, n, large)


def _pool_mixer(u, pool_w, pool_scale):
    B, S, _ = u.shape
    uf = u.astype(jnp.float32)
    cs = jnp.concatenate([jnp.zeros((B, 1, POOL_WIDTH), jnp.float32), jnp.cumsum(uf, axis=1)], axis=1)
    t = jnp.arange(S)
    groups = []
    for g, win in enumerate(POOL_WINDOWS):
        c0, c1 = g * POOL_GROUP, (g + 1) * POOL_GROUP
        lo = jnp.maximum(t - win // 2, 0)
        hi = jnp.minimum(t + win // 2, S)
        csg = cs[..., c0:c1]
        win_sum = jnp.take(csg, hi, axis=1) - jnp.take(csg, lo, axis=1)
        count = (hi - lo).astype(jnp.float32)[None, :, None]
        groups.append(win_sum / count - uf[..., c0:c1])
    p = jnp.stack(groups, axis=2).astype(u.dtype)
    y = jnp.einsum('bsgc,gcd->bsgd', p, pool_w).reshape(B, S, POOL_WIDTH)
    return y * pool_scale


def _mla(q_lat, kv_lat, k_rope, q_a_norm, w_q_up, kv_a_norm, w_kv_up):
    B, S, _ = q_lat.shape
    q = (_rms(q_lat, q_a_norm) @ w_q_up).reshape(B, S, MLA_HEADS, MLA_NOPE + MLA_ROPE)
    kv = (_rms(kv_lat, kv_a_norm) @ w_kv_up).reshape(B, S, MLA_HEADS, MLA_NOPE + MLA_V)
    q_nope, q_pe = q[..., :MLA_NOPE], q[..., MLA_NOPE:]
    k_nope, v = kv[..., :MLA_NOPE], kv[..., MLA_NOPE:]
    cos, sin = _rope_angles(jnp.arange(S), MLA_ROPE)
    q_pe = _apply_rope(q_pe, cos, sin)
    k_pe = _apply_rope(k_rope[:, :, None, :], cos, sin)[:, :, 0, :]
    scale = (MLA_NOPE + MLA_ROPE) ** -0.5

    def block(start):
        qn = lax.dynamic_slice_in_dim(q_nope, start, Q_BLOCK, axis=1)
        qp = lax.dynamic_slice_in_dim(q_pe, start, Q_BLOCK, axis=1)
        s = (jnp.einsum('bqhd,bkhd->bhqk', qn, k_nope)
             + jnp.einsum('bqhd,bkd->bhqk', qp, k_pe)).astype(jnp.float32) * scale
        p = jax.nn.softmax(s, axis=-1).astype(v.dtype)
        return jnp.einsum('bhqk,bkhd->bqhd', p, v)

    o = _sweep(block, S)
    return o.reshape(B, S, MLA_HEADS * MLA_V)


def _diff_attention(cq, ck, cv, lambda_q1, lambda_k1, lambda_q2, lambda_k2, diff_subln, rel_bias, lambda_init):
    B, S, _ = cq.shape
    q = cq.reshape(B, S, DIFF_HEADS, 2, DIFF_HD)
    k = ck.reshape(B, S, DIFF_HEADS, 2, DIFF_HD)
    v = cv.reshape(B, S, DIFF_HEADS, 2 * DIFF_HD)
    q1, q2 = q[:, :, :, 0], q[:, :, :, 1]
    k1, k2 = k[:, :, :, 0], k[:, :, :, 1]
    lam = (jnp.exp(jnp.sum(lambda_q1.astype(jnp.float32) * lambda_k1.astype(jnp.float32)))
           - jnp.exp(jnp.sum(lambda_q2.astype(jnp.float32) * lambda_k2.astype(jnp.float32)))
           + lambda_init)
    scale = DIFF_HD ** -0.5
    k_pos = jnp.arange(S)

    def block(start):
        q1b = lax.dynamic_slice_in_dim(q1, start, Q_BLOCK, axis=1)
        q2b = lax.dynamic_slice_in_dim(q2, start, Q_BLOCK, axis=1)
        q_pos = start + jnp.arange(Q_BLOCK)
        bucket = _t5_bucket(k_pos[None, :] - q_pos[:, None])
        bias = jnp.transpose(rel_bias[bucket].astype(jnp.float32), (2, 0, 1))[None]
        s1 = jnp.einsum('bqhd,bkhd->bhqk', q1b, k1).astype(jnp.float32) * scale + bias
        s2 = jnp.einsum('bqhd,bkhd->bhqk', q2b, k2).astype(jnp.float32) * scale + bias
        p = jax.nn.softmax(s1, axis=-1) - lam * jax.nn.softmax(s2, axis=-1)
        return jnp.einsum('bhqk,bkhd->bqhd', p.astype(v.dtype), v)

    o = _sweep(block, S)
    o = _rms(o, diff_subln) * (1.0 - lambda_init)
    return o.reshape(B, S, DIFF_HEADS * 2 * DIFF_HD)


def _axial_rope(x, cr, sr, cc, sc):
    return jnp.concatenate([_apply_rope(x[..., :AXIAL_DIM], cr, sr),
                            _apply_rope(x[..., AXIAL_DIM:], cc, sc)], axis=-1)


def _axial_gqa(dq, dk, dv, gqa_q_norm, gqa_k_norm):
    B, S, _ = dq.shape
    rows = S // GRID_W
    q = _rms(dq.reshape(B, S, GQA_HEADS, GQA_HD), gqa_q_norm)
    k = _rms(dk.reshape(B, S, GQA_KV, GQA_HD), gqa_k_norm)
    v = dv.reshape(B, S, GQA_KV, GQA_HD)
    row = jnp.repeat(jnp.arange(rows), GRID_W)
    col = jnp.tile(jnp.arange(GRID_W), rows)
    cr, sr = _rope_angles(row, AXIAL_DIM)
    cc, sc = _rope_angles(col, AXIAL_DIM)
    q = _axial_rope(q, cr, sr, cc, sc).reshape(B, S, GQA_KV, GQA_GROUP, GQA_HD)
    k = _axial_rope(k, cr, sr, cc, sc)
    scale = GQA_HD ** -0.5

    def block(start):
        qb = lax.dynamic_slice_in_dim(q, start, Q_BLOCK, axis=1)
        s = jnp.einsum('bqngd,bsnd->bngqs', qb, k).astype(jnp.float32) * scale
        p = jax.nn.softmax(s, axis=-1).astype(v.dtype)
        return jnp.einsum('bngqs,bsnd->bqngd', p, v)

    o = _sweep(block, S)
    return o.reshape(B, S, GQA_HEADS * GQA_HD)


def _mixer_even(h, w_in, pool_w, pool_scale, q_a_norm, w_q_up, kv_a_norm, w_kv_up, w_out):
    u = h @ w_in
    o1 = POOL_WIDTH
    o2 = o1 + MLA_Q_LORA
    o3 = o2 + MLA_KV_LORA
    a = _pool_mixer(u[..., :o1], pool_w, pool_scale)
    b = _mla(u[..., o1:o2], u[..., o2:o3], u[..., o3:], q_a_norm, w_q_up, kv_a_norm, w_kv_up)
    return jnp.concatenate([a, b], axis=-1) @ w_out


def _mixer_odd(h, w_in, lambda_q1, lambda_k1, lambda_q2, lambda_k2, diff_subln, rel_bias,
               gqa_q_norm, gqa_k_norm, w_out, lambda_init):
    u = h @ w_in
    cw = DIFF_HEADS * 2 * DIFF_HD
    o1, o2, o3 = cw, 2 * cw, 3 * cw
    o4 = o3 + GQA_HEADS * GQA_HD
    o5 = o4 + GQA_KV * GQA_HD
    c = _diff_attention(u[..., :o1], u[..., o1:o2], u[..., o2:o3], lambda_q1, lambda_k1,
                        lambda_q2, lambda_k2, diff_subln, rel_bias, lambda_init)
    d = _axial_gqa(u[..., o3:o4], u[..., o4:o5], u[..., o5:], gqa_q_norm, gqa_k_norm)
    return jnp.concatenate([c, d], axis=-1) @ w_out


def _swiglu(h, w_gate, w_up, w_down):
    return (jax.nn.silu(h @ w_gate) * (h @ w_up)) @ w_down


def _moe(h, router_w, router_b, w_gate, w_up, w_down):
    logits = (h @ router_w).astype(jnp.float32) + router_b.astype(jnp.float32)
    top_val, top_idx = lax.top_k(logits, TOP_K)
    gates = jax.nn.softmax(top_val, axis=-1)
    comb = jnp.sum(jax.nn.one_hot(top_idx, N_EXPERTS, dtype=jnp.float32) * gates[..., None], axis=-2)
    y = jnp.zeros_like(h)
    for e in range(N_EXPERTS):
        y = y + comb[..., e:e + 1].astype(h.dtype) * _swiglu(h, w_gate[e], w_up[e], w_down[e])
    return y


def _forward(x, rel_bias, norm_mix0, w_in0, pool_w, pool_scale, q_a_norm, w_q_up, kv_a_norm,
             w_kv_up, w_out0, norm_ffn0, ffn_w_gate, ffn_w_up, ffn_w_down, norm_mix1, w_in1,
             lambda_q1, lambda_k1, lambda_q2, lambda_k2, diff_subln, gqa_q_norm, gqa_k_norm,
             w_out1, norm_ffn1, router_w, router_b, moe_w_gate, moe_w_up, moe_w_down, final_norm):
    for layer in range(DEPTH):
        i = layer // 2
        if layer % 2 == 0:
            x = x + _mixer_even(_rms(x, norm_mix0[i]), w_in0[i], pool_w[i], pool_scale[i],
                                q_a_norm[i], w_q_up[i], kv_a_norm[i], w_kv_up[i], w_out0[i])
            x = x + _swiglu(_rms(x, norm_ffn0[i]), ffn_w_gate[i], ffn_w_up[i], ffn_w_down[i])
        else:
            lambda_init = 0.8 - 0.6 * math.exp(-0.3 * layer)
            x = x + _mixer_odd(_rms(x, norm_mix1[i]), w_in1[i], lambda_q1[i], lambda_k1[i],
                               lambda_q2[i], lambda_k2[i], diff_subln[i], rel_bias,
                               gqa_q_norm[i], gqa_k_norm[i], w_out1[i], lambda_init)
            x = x + _moe(_rms(x, norm_ffn1[i]), router_w[i], router_b[i],
                         moe_w_gate[i], moe_w_up[i], moe_w_down[i])
    return _rms(x, final_norm)


def setup_inputs(seed: int = 0) -> dict:
    key = jax.random.key(seed)
    ks = jax.random.split(key, 40)
    f32 = jnp.float32
    D = D_MODEL
    E = N_EVEN
    O = N_ODD

    def nrm(i, shape, scale):
        return jax.random.normal(ks[i], shape, f32) * scale

    def gain(i, shape, noise=0.05):
        return 1.0 + noise * jax.random.normal(ks[i], shape, f32)

    return {
        'x_prompt': nrm(0, (BATCH, SEQ, D), 1.0),
        'x_sample': nrm(1, (DEC_BATCH, DEC_SEQ, D), 1.0),
        'rel_bias': nrm(2, (REL_BUCKETS, DIFF_HEADS), 0.5),
        'norm_mix0': gain(3, (E, D)),
        'w_in0': nrm(4, (E, D, IN0_WIDTH), D ** -0.5),
        'pool_w': nrm(5, (E, len(POOL_WINDOWS), POOL_GROUP, POOL_GROUP), POOL_GROUP ** -0.5),
        'pool_scale': gain(6, (E, POOL_WIDTH), 0.1),
        'q_a_norm': gain(7, (E, MLA_Q_LORA)),
        'w_q_up': nrm(8, (E, MLA_Q_LORA, MLA_HEADS * (MLA_NOPE + MLA_ROPE)), MLA_Q_LORA ** -0.5),
        'kv_a_norm': gain(9, (E, MLA_KV_LORA)),
        'w_kv_up': nrm(10, (E, MLA_KV_LORA, MLA_HEADS * (MLA_NOPE + MLA_V)), MLA_KV_LORA ** -0.5),
        'w_out0': nrm(11, (E, MIX0_WIDTH, D), MIX0_WIDTH ** -0.5),
        'norm_ffn0': gain(12, (E, D)),
        'ffn_w_gate': nrm(13, (E, D, FFN_DIM), D ** -0.5),
        'ffn_w_up': nrm(14, (E, D, FFN_DIM), D ** -0.5),
        'ffn_w_down': nrm(15, (E, FFN_DIM, D), FFN_DIM ** -0.5),
        'norm_mix1': gain(16, (O, D)),
        'w_in1': nrm(17, (O, D, IN1_WIDTH), D ** -0.5),
        'lambda_q1': nrm(18, (O, DIFF_HD), 0.1),
        'lambda_k1': nrm(19, (O, DIFF_HD), 0.1),
        'lambda_q2': nrm(20, (O, DIFF_HD), 0.1),
        'lambda_k2': nrm(21, (O, DIFF_HD), 0.1),
        'diff_subln': gain(22, (O, 2 * DIFF_HD)),
        'gqa_q_norm': gain(23, (O, GQA_HD)),
        'gqa_k_norm': gain(24, (O, GQA_HD)),
        'w_out1': nrm(25, (O, MIX1_WIDTH, D), MIX1_WIDTH ** -0.5),
        'norm_ffn1': gain(26, (O, D)),
        'router_w': nrm(27, (O, D, N_EXPERTS), D ** -0.5),
        'router_b': nrm(28, (O, N_EXPERTS), 0.01),
        'moe_w_gate': nrm(29, (O, N_EXPERTS, D, EXPERT_DIM), D ** -0.5),
        'moe_w_up': nrm(30, (O, N_EXPERTS, D, EXPERT_DIM), D ** -0.5),
        'moe_w_down': nrm(31, (O, N_EXPERTS, EXPERT_DIM, D), EXPERT_DIM ** -0.5),
        'final_norm': gain(32, (D,)),
    }


def reference(x_prompt, x_sample, rel_bias, norm_mix0, w_in0, pool_w, pool_scale, q_a_norm, w_q_up,
              kv_a_norm, w_kv_up, w_out0, norm_ffn0, ffn_w_gate, ffn_w_up, ffn_w_down, norm_mix1,
              w_in1, lambda_q1, lambda_k1, lambda_q2, lambda_k2, diff_subln, gqa_q_norm, gqa_k_norm,
              w_out1, norm_ffn1, router_w, router_b, moe_w_gate, moe_w_up, moe_w_down, final_norm):
    weights = (rel_bias, norm_mix0, w_in0, pool_w, pool_scale, q_a_norm, w_q_up, kv_a_norm,
               w_kv_up, w_out0, norm_ffn0, ffn_w_gate, ffn_w_up, ffn_w_down, norm_mix1, w_in1,
               lambda_q1, lambda_k1, lambda_q2, lambda_k2, diff_subln, gqa_q_norm, gqa_k_norm,
               w_out1, norm_ffn1, router_w, router_b, moe_w_gate, moe_w_up, moe_w_down, final_norm)
    y_prompt = _forward(x_prompt, *weights)
    y_sample = _forward(x_sample, *weights)
    return (y_prompt, y_sample)
```

```python
import functools
import math

import jax
import jax.numpy as jnp
from jax import lax
from jax.experimental import pallas as pl
from jax.experimental.pallas import tpu as pltpu

D_MODEL = 2048
RMS_EPS = 1e-6
ROPE_THETA = 10000.0
GRID_W = 64
POOL_WINDOWS = (2, 4, 8, 16)
POOL_GROUP = D_MODEL // 8
POOL_WIDTH = POOL_GROUP * len(POOL_WINDOWS)
MLA_HEADS = 8
MLA_Q_LORA = D_MODEL // 4
MLA_KV_LORA = D_MODEL // 8
MLA_NOPE = 128
MLA_ROPE = 64
MLA_V = 128
MLA_QK_PAD = 256
IN0_WIDTH = POOL_WIDTH + MLA_Q_LORA + MLA_KV_LORA + MLA_ROPE
IN0_PAD = 1920
DIFF_HEADS = 8
DIFF_HD = 64
GQA_HEADS = 8
GQA_KV = 2
GQA_GROUP = GQA_HEADS // GQA_KV
GQA_HD = 128
AXIAL_DIM = GQA_HD // 2
REL_BUCKETS = 32
REL_MAX_DIST = 128
N_EXPERTS = 8
TOP_K = 2
LANES = 128
HALO = 16

_F32 = jnp.float32
_BF16 = jnp.bfloat16
_VMEM_LIMIT = 56 * 1024 * 1024
_NEG = -1e30


def _cparams(*sem):
    return pltpu.CompilerParams(dimension_semantics=sem, vmem_limit_bytes=_VMEM_LIMIT)


def _tile(n, pref):
    t = pref
    while n % t:
        t //= 2
    return t


def _rms(x, g):
    ms = jnp.mean(x * x, axis=-1, keepdims=True)
    return x * lax.rsqrt(ms + RMS_EPS) * g


def _rope_partner(x):
    lane = lax.broadcasted_iota(jnp.int32, x.shape, 1)
    first = (lane % 64) < 32
    return jnp.where(first, pltpu.roll(x, 96, 1), pltpu.roll(x, 32, 1))


def _rms_matmul_kernel(x_ref, g_ref, w_ref, o_ref, h_sc):
    @pl.when(pl.program_id(1) == 0)
    def _():
        h_sc[...] = _rms(x_ref[...], g_ref[...]).astype(_BF16)

    o_ref[...] = jnp.dot(h_sc[...], w_ref[...], preferred_element_type=_F32).astype(o_ref.dtype)


def _rms_matmul(x, g, w, tn):
    n, d = x.shape
    width = w.shape[1]
    tm = _tile(n, 1024)
    return pl.pallas_call(
        _rms_matmul_kernel,
        grid=(n // tm, width // tn),
        in_specs=[
            pl.BlockSpec((tm, d), lambda i, j: (i, 0)),
            pl.BlockSpec((1, d), lambda i, j: (0, 0)),
            pl.BlockSpec((d, tn), lambda i, j: (0, j)),
        ],
        out_specs=pl.BlockSpec((tm, tn), lambda i, j: (i, j)),
        out_shape=jax.ShapeDtypeStruct((n, width), _BF16),
        scratch_shapes=[pltpu.VMEM((tm, d), _BF16)],
        compiler_params=_cparams("parallel", "arbitrary"),
        name="rms_matmul",
    )(x, g, w)


def _out_proj_kernel(a_ref, b_ref, w_ref, x_ref, o_ref):
    ka = a_ref.shape[1]
    acc = jnp.dot(a_ref[...], w_ref[:ka, :], preferred_element_type=_F32)
    acc = acc + jnp.dot(b_ref[...], w_ref[ka:, :], preferred_element_type=_F32)
    o_ref[...] = x_ref[...] + acc


def _out_proj(a, b, w, x):
    n, d = x.shape
    tm = _tile(n, 1024)
    tn = 512
    return pl.pallas_call(
        _out_proj_kernel,
        grid=(n // tm, d // tn),
        in_specs=[
            pl.BlockSpec((tm, a.shape[1]), lambda i, j: (i, 0)),
            pl.BlockSpec((tm, b.shape[1]), lambda i, j: (i, 0)),
            pl.BlockSpec((w.shape[0], tn), lambda i, j: (0, j)),
            pl.BlockSpec((tm, tn), lambda i, j: (i, j)),
        ],
        out_specs=pl.BlockSpec((tm, tn), lambda i, j: (i, j)),
        out_shape=jax.ShapeDtypeStruct((n, d), _F32),
        compiler_params=_cparams("parallel", "arbitrary"),
        name="out_proj",
    )(a, b, w, x)


def _swiglu_acc(h, wg, wu, wd):
    gate = jnp.dot(h, wg, preferred_element_type=_F32)
    up = jnp.dot(h, wu, preferred_element_type=_F32)
    act = (gate / (1.0 + jnp.exp(-gate)) * up).astype(_BF16)
    return jnp.dot(act, wd, preferred_element_type=_F32)


def _ffn_kernel(x_ref, g_ref, wg_ref, wu_ref, wd_ref, o_ref, h_sc):
    @pl.when(pl.program_id(1) == 0)
    def _():
        x = x_ref[...]
        h_sc[...] = _rms(x, g_ref[...]).astype(_BF16)
        o_ref[...] = x

    o_ref[...] += _swiglu_acc(h_sc[...], wg_ref[...], wu_ref[...], wd_ref[...])


def _ffn(x, g, wg, wu, wd):
    n, d = x.shape
    f = wg.shape[1]
    tm = _tile(n, 512)
    tf = 512
    return pl.pallas_call(
        _ffn_kernel,
        grid=(n // tm, f // tf),
        in_specs=[
            pl.BlockSpec((tm, d), lambda i, j: (i, 0)),
            pl.BlockSpec((1, d), lambda i, j: (0, 0)),
            pl.BlockSpec((d, tf), lambda i, j: (0, j)),
            pl.BlockSpec((d, tf), lambda i, j: (0, j)),
            pl.BlockSpec((tf, d), lambda i, j: (j, 0)),
        ],
        out_specs=pl.BlockSpec((tm, d), lambda i, j: (i, 0)),
        out_shape=jax.ShapeDtypeStruct((n, d), _F32),
        scratch_shapes=[pltpu.VMEM((tm, d), _BF16)],
        compiler_params=_cparams("parallel", "arbitrary"),
        name="ffn_swiglu",
    )(x, g, wg, wu, wd)


def _pool_kernel(pos0_ref, slen_ref, prev_ref, cur_ref, next_ref, w_ref, scale_ref, o_ref, buf_sc):
    i = pl.program_id(0)
    t = cur_ref.shape[0]
    pos0 = pos0_ref[i]
    slen = slen_ref[i]
    cur = cur_ref[...].astype(_F32)
    buf_sc[0:HALO, :] = jnp.where(pos0 > 0, prev_ref[...].astype(_F32), 0.0)
    buf_sc[HALO:HALO + t, :] = cur
    buf_sc[HALO + t:, :] = jnp.where(pos0 + t < slen, next_ref[...].astype(_F32), 0.0)
    tpos = pos0 + lax.broadcasted_iota(jnp.int32, (t, 1), 0)
    for g, win in enumerate(POOL_WINDOWS):
        c0, c1 = g * POOL_GROUP, (g + 1) * POOL_GROUP
        half = win // 2
        acc = buf_sc[HALO - half:HALO - half + t, c0:c1]
        for dlt in range(-half + 1, half):
            acc = acc + buf_sc[HALO + dlt:HALO + dlt + t, c0:c1]
        count = (jnp.minimum(tpos + half, slen) - jnp.maximum(tpos - half, 0)).astype(_F32)
        p = (acc / count - cur[:, c0:c1]).astype(_BF16)
        y = jnp.dot(p, w_ref[g], preferred_element_type=_F32)
        o_ref[:, c0:c1] = (y * scale_ref[:, c0:c1]).astype(o_ref.dtype)


def _pool_mixer(u, pool_w, pool_scale, tile_pos0, tile_slen, t):
    n = u.shape[0]
    hb = t // HALO
    last = n // HALO - 1
    grid_spec = pltpu.PrefetchScalarGridSpec(
        num_scalar_prefetch=2,
        grid=(n // t,),
        in_specs=[
            pl.BlockSpec((HALO, POOL_WIDTH), lambda i, p, s: (jnp.maximum(i * hb - 1, 0), 0)),
            pl.BlockSpec((t, POOL_WIDTH), lambda i, p, s: (i, 0)),
            pl.BlockSpec((HALO, POOL_WIDTH), lambda i, p, s: (jnp.minimum((i + 1) * hb, last), 0)),
            pl.BlockSpec((len(POOL_WINDOWS), POOL_GROUP, POOL_GROUP), lambda i, p, s: (0, 0, 0)),
            pl.BlockSpec((1, POOL_WIDTH), lambda i, p, s: (0, 0)),
        ],
        out_specs=pl.BlockSpec((t, POOL_WIDTH), lambda i, p, s: (i, 0)),
        scratch_shapes=[pltpu.VMEM((t + 2 * HALO, POOL_WIDTH), _F32)],
    )
    return pl.pallas_call(
        _pool_kernel,
        grid_spec=grid_spec,
        out_shape=jax.ShapeDtypeStruct((n, POOL_WIDTH), _BF16),
        compiler_params=_cparams("arbitrary"),
        name="pool_mixer",
    )(tile_pos0, tile_slen, u, u, u, pool_w, pool_scale)


def _mla_prep_kernel(ql_ref, kvl_ref, kr_ref, gq_ref, gkv_ref, wq_ref, wkv_ref, cos_ref, sin_ref,
                     q_ref, k_ref, v_ref):
    scale = (MLA_NOPE + MLA_ROPE) ** -0.5
    cos = cos_ref[...]
    sin = sin_ref[...]
    lane = lax.broadcasted_iota(jnp.int32, cos.shape, 1)
    low = lane < MLA_ROPE

    qn = _rms(ql_ref[...].astype(_F32), gq_ref[...]).astype(_BF16)
    q = jnp.dot(qn, wq_ref[...], preferred_element_type=_F32)
    kvn = _rms(kvl_ref[...].astype(_F32), gkv_ref[...]).astype(_BF16)
    kv = jnp.dot(kvn, wkv_ref[...], preferred_element_type=_F32)
    nope_w = MLA_HEADS * MLA_NOPE
    v_ref[...] = kv[:, nope_w:].astype(v_ref.dtype)

    kr = kr_ref[...].astype(_F32)
    k_pe = jnp.where(low, kr * cos + _rope_partner(kr) * sin, 0.0).astype(k_ref.dtype)
    for pair in range(MLA_HEADS // 2):
        pe = q[:, nope_w + pair * LANES:nope_w + (pair + 1) * LANES]
        roped = (pe * cos + _rope_partner(pe) * sin) * scale
        for sub in range(2):
            h = 2 * pair + sub
            pe_h = roped if sub == 0 else pltpu.roll(roped, MLA_ROPE, 1)
            base = h * MLA_QK_PAD
            q_ref[:, base:base + MLA_NOPE] = (q[:, h * MLA_NOPE:(h + 1) * MLA_NOPE] * scale).astype(q_ref.dtype)
            q_ref[:, base + MLA_NOPE:base + MLA_QK_PAD] = jnp.where(low, pe_h, 0.0).astype(q_ref.dtype)
            k_ref[:, base:base + MLA_NOPE] = kv[:, h * MLA_NOPE:(h + 1) * MLA_NOPE].astype(k_ref.dtype)
            k_ref[:, base + MLA_NOPE:base + MLA_QK_PAD] = k_pe


def _mla_prep(u, gq, gkv, wq, wkv, cos_t, sin_t):
    n = u.shape[0]
    tm = _tile(n, 512)
    qk_w = MLA_HEADS * MLA_QK_PAD
    v_w = MLA_HEADS * MLA_V
    return pl.pallas_call(
        _mla_prep_kernel,
        grid=(n // tm,),
        in_specs=[
            pl.BlockSpec((tm, MLA_Q_LORA), lambda i: (i, POOL_WIDTH // MLA_Q_LORA)),
            pl.BlockSpec((tm, MLA_KV_LORA), lambda i: (i, (POOL_WIDTH + MLA_Q_LORA) // MLA_KV_LORA)),
            pl.BlockSpec((tm, LANES), lambda i: (i, (POOL_WIDTH + MLA_Q_LORA + MLA_KV_LORA) // LANES)),
            pl.BlockSpec((1, MLA_Q_LORA), lambda i: (0, 0)),
            pl.BlockSpec((1, MLA_KV_LORA), lambda i: (0, 0)),
            pl.BlockSpec(wq.shape, lambda i: (0, 0)),
            pl.BlockSpec(wkv.shape, lambda i: (0, 0)),
            pl.BlockSpec((tm, LANES), lambda i: (i, 0)),
            pl.BlockSpec((tm, LANES), lambda i: (i, 0)),
        ],
        out_specs=[
            pl.BlockSpec((tm, qk_w), lambda i: (i, 0)),
            pl.BlockSpec((tm, qk_w), lambda i: (i, 0)),
            pl.BlockSpec((tm, v_w), lambda i: (i, 0)),
        ],
        out_shape=[
            jax.ShapeDtypeStruct((n, qk_w), _BF16),
            jax.ShapeDtypeStruct((n, qk_w), _BF16),
            jax.ShapeDtypeStruct((n, v_w), _BF16),
        ],
        compiler_params=_cparams("parallel"),
        name="mla_prep",
    )(u, u, u, gq, gkv, wq, wkv, cos_t, sin_t)


def _gqa_prep_kernel(dq_ref, dk_ref, gq_ref, gk_ref, cos_ref, sin_ref, q_ref, k_ref):
    scale = GQA_HD ** -0.5
    cos = cos_ref[...]
    sin = sin_ref[...]

    def norm_rope(x, g):
        xn = _rms(x.astype(_F32), g)
        return xn * cos + _rope_partner(xn) * sin

    for h in range(GQA_HEADS):
        sl = slice(h * GQA_HD, (h + 1) * GQA_HD)
        q_ref[:, sl] = (norm_rope(dq_ref[:, sl], gq_ref[...]) * scale).astype(q_ref.dtype)
    for h in range(GQA_KV):
        sl = slice(h * GQA_HD, (h + 1) * GQA_HD)
        k_ref[:, sl] = norm_rope(dk_ref[:, sl], gk_ref[...]).astype(k_ref.dtype)


def _gqa_prep(u, gq, gk, cos_t, sin_t):
    n = u.shape[0]
    tm = _tile(n, 512)
    q_w = GQA_HEADS * GQA_HD
    k_w = GQA_KV * GQA_HD
    dq_off = 3 * DIFF_HEADS * 2 * DIFF_HD
    return pl.pallas_call(
        _gqa_prep_kernel,
        grid=(n // tm,),
        in_specs=[
            pl.BlockSpec((tm, q_w), lambda i: (i, dq_off // q_w)),
            pl.BlockSpec((tm, k_w), lambda i: (i, (dq_off + q_w) // k_w)),
            pl.BlockSpec((1, GQA_HD), lambda i: (0, 0)),
            pl.BlockSpec((1, GQA_HD), lambda i: (0, 0)),
            pl.BlockSpec((tm, LANES), lambda i: (i, 0)),
            pl.BlockSpec((tm, LANES), lambda i: (i, 0)),
        ],
        out_specs=[
            pl.BlockSpec((tm, q_w), lambda i: (i, 0)),
            pl.BlockSpec((tm, k_w), lambda i: (i, 0)),
        ],
        out_shape=[
            jax.ShapeDtypeStruct((n, q_w), _BF16),
            jax.ShapeDtypeStruct((n, k_w), _BF16),
        ],
        compiler_params=_cparams("parallel"),
        name="gqa_prep",
    )(u, u, gq, gk, cos_t, sin_t)


def _flash_kernel(*refs, diff, lambda_init):
    if diff:
        (q_ref, k_ref, v_ref, bias_ref, lq1_ref, lk1_ref, lq2_ref, lk2_ref, sub_ref, _, o_ref,
         q_sc, m_sc, l_sc, acc_sc) = refs
    else:
        q_ref, k_ref, v_ref, _, o_ref, m_sc, l_sc, acc_sc = refs
    ki = pl.program_id(3)
    tq = q_ref.shape[0]

    @pl.when(ki == 0)
    def _():
        m_sc[...] = jnp.full_like(m_sc, _NEG)
        l_sc[...] = jnp.zeros_like(l_sc)
        acc_sc[...] = jnp.zeros_like(acc_sc)
        if diff:
            q = q_ref[...]
            lane = lax.broadcasted_iota(jnp.int32, q.shape, 1)
            zero = jnp.zeros_like(q)
            q_sc[0:tq, :] = jnp.where(lane < DIFF_HD, q, zero)
            q_sc[tq:, :] = jnp.where(lane >= DIFF_HD, q, zero)

    q = q_sc[...] if diff else q_ref[...]
    s = lax.dot_general(q, k_ref[...], (((1,), (1,)), ((), ())), preferred_element_type=_F32)
    if diff:
        bias = bias_ref[...]
        s = s + jnp.concatenate([bias, bias], axis=0)
    m_prev = m_sc[...]
    m_new = jnp.maximum(m_prev, jnp.max(s, axis=-1, keepdims=True))
    alpha = jnp.exp(m_prev - m_new)
    p = jnp.exp(s - m_new)
    l_sc[...] = alpha * l_sc[...] + jnp.sum(p, axis=-1, keepdims=True)
    acc_sc[...] = alpha * acc_sc[...] + jnp.dot(p.astype(_BF16), v_ref[...], preferred_element_type=_F32)
    m_sc[...] = m_new

    @pl.when(ki == pl.num_programs(3) - 1)
    def _():
        o = acc_sc[...] / l_sc[...]
        if diff:
            lam = (jnp.exp(jnp.sum(lq1_ref[...] * lk1_ref[...], axis=-1, keepdims=True))
                   - jnp.exp(jnp.sum(lq2_ref[...] * lk2_ref[...], axis=-1, keepdims=True))
                   + lambda_init)
            o = o[0:tq, :] - lam * o[tq:, :]
            o = _rms(o, sub_ref[...]) * (1.0 - lambda_init)
        o_ref[...] = o.astype(o_ref.dtype)


def _flash(q_arr, k_arr, v_arr, prev_out, *, row0, n_seq, seq, heads, kv_group, dq, dv,
           q_col0, k_col0, v_col0, t, diff_args=None, lambda_init=0.0, name):
    n = q_arr.shape[0]
    nt = seq // t
    blk0 = row0 // t
    diff = diff_args is not None
    rows = 2 * t if diff else t

    def q_map(b, h, qi, ki):
        return (blk0 + b * nt + qi, q_col0 + h)

    def k_map(b, h, qi, ki):
        return (blk0 + b * nt + ki, k_col0 + h // kv_group)

    def v_map(b, h, qi, ki):
        return (blk0 + b * nt + ki, v_col0 + h // kv_group)

    def o_map(b, h, qi, ki):
        return (blk0 + b * nt + qi, h)

    in_specs = [pl.BlockSpec((t, dq), q_map), pl.BlockSpec((t, dq), k_map), pl.BlockSpec((t, dv), v_map)]
    args = [q_arr, k_arr, v_arr]
    scratch = []
    if diff:
        bias, lq1, lk1, lq2, lk2, sub = diff_args
        in_specs.append(pl.BlockSpec((None, None, t, t),
                                     lambda b, h, qi, ki: (h, jnp.clip(ki - qi, -2, 2) + 2, 0, 0)))
        in_specs += [pl.BlockSpec((1, DIFF_HD), lambda b, h, qi, ki: (0, 0))] * 4
        in_specs.append(pl.BlockSpec((1, 2 * DIFF_HD), lambda b, h, qi, ki: (0, 0)))
        args += [bias, lq1, lk1, lq2, lk2, sub]
        scratch.append(pltpu.VMEM((rows, dq), _BF16))
    out_shape = jax.ShapeDtypeStruct((n, heads * dv), _BF16)
    if prev_out is None:
        prev_out = jnp.zeros(out_shape.shape, out_shape.dtype)
    aliases = {len(args): 0}
    in_specs.append(pl.BlockSpec(memory_space=pl.ANY))
    args.append(prev_out)
    scratch += [pltpu.VMEM((rows, 1), _F32), pltpu.VMEM((rows, 1), _F32), pltpu.VMEM((rows, dv), _F32)]
    return pl.pallas_call(
        functools.partial(_flash_kernel, diff=diff, lambda_init=lambda_init),
        grid=(n_seq, heads, nt, nt),
        in_specs=in_specs,
        out_specs=pl.BlockSpec((t, dv), o_map),
        out_shape=out_shape,
        scratch_shapes=scratch,
        input_output_aliases=aliases,
        compiler_params=_cparams("parallel", "parallel", "parallel", "arbitrary"),
        name=name,
    )(*args)


def _router_kernel(x_ref, g_ref, w_ref, b_ref, h_ref, route_ref):
    h = _rms(x_ref[...], g_ref[...])
    h_ref[...] = h
    logits = jnp.dot(h, w_ref[...], preferred_element_type=_F32, precision=lax.Precision.HIGHEST) + b_ref[...]
    lane = lax.broadcasted_iota(jnp.int32, logits.shape, 1)
    logits = jnp.where(lane < N_EXPERTS, logits, _NEG)
    v1 = jnp.max(logits, axis=-1, keepdims=True)
    i1 = jnp.min(jnp.where(logits == v1, lane, LANES), axis=-1, keepdims=True)
    rest = jnp.where(lane == i1, _NEG, logits)
    v2 = jnp.max(rest, axis=-1, keepdims=True)
    i2 = jnp.min(jnp.where(rest == v2, lane, LANES), axis=-1, keepdims=True)
    g1 = 1.0 / (1.0 + jnp.exp(v2 - v1))
    g2 = 1.0 - g1
    route = jnp.where(lane == 0, i1.astype(_F32),
                      jnp.where(lane == 1, i2.astype(_F32),
                                jnp.where(lane == 2, g1, jnp.where(lane == 3, g2, 0.0))))
    route_ref[...] = route


def _router(x, g, w_pad, b_pad):
    n, d = x.shape
    tm = _tile(n, 512)
    return pl.pallas_call(
        _router_kernel,
        grid=(n // tm,),
        in_specs=[
            pl.BlockSpec((tm, d), lambda i: (i, 0)),
            pl.BlockSpec((1, d), lambda i: (0, 0)),
            pl.BlockSpec((d, LANES), lambda i: (0, 0)),
            pl.BlockSpec((1, LANES), lambda i: (0, 0)),
        ],
        out_specs=[pl.BlockSpec((tm, d), lambda i: (i, 0)), pl.BlockSpec((tm, LANES), lambda i: (i, 0))],
        out_shape=[jax.ShapeDtypeStruct((n, d), _F32), jax.ShapeDtypeStruct((n, LANES), _F32)],
        compiler_params=_cparams("parallel"),
        name="moe_router",
    )(x, g, w_pad, b_pad)


def _row_copy(src, src_row, dst, dst_row, sem):
    return pltpu.make_async_copy(src.at[pl.ds(src_row, 1), :], dst.at[pl.ds(dst_row, 1), :], sem)


def _dispatch_kernel(pos_ref, h_ref, xs_in_ref, xs_ref, sem):
    del xs_in_ref
    c = pl.program_id(0)
    tc = pos_ref.shape[2] // TOP_K

    def start(i, carry):
        for k in range(TOP_K):
            _row_copy(h_ref, c * tc + i, xs_ref, pos_ref[0, 0, TOP_K * i + k], sem).start()
        return carry

    lax.fori_loop(0, tc, start, 0)

    def wait(i, carry):
        for k in range(TOP_K):
            _row_copy(h_ref, 0, xs_ref, 0, sem).wait()
        return carry

    lax.fori_loop(0, tc, wait, 0)


def _dispatch(h, pos3, xs_zero):
    n_chunks, _, width = pos3.shape
    return pl.pallas_call(
        _dispatch_kernel,
        grid=(n_chunks,),
        in_specs=[
            pl.BlockSpec((1, 1, width), lambda c: (c, 0, 0), memory_space=pltpu.SMEM),
            pl.BlockSpec(memory_space=pl.ANY),
            pl.BlockSpec(memory_space=pl.ANY),
        ],
        out_specs=pl.BlockSpec(memory_space=pl.ANY),
        out_shape=jax.ShapeDtypeStruct(xs_zero.shape, xs_zero.dtype),
        scratch_shapes=[pltpu.SemaphoreType.DMA(())],
        input_output_aliases={2: 0},
        compiler_params=_cparams("arbitrary"),
        name="moe_dispatch",
    )(pos3, h, xs_zero)


def _moe_ffn_kernel(te_ref, nused_ref, xs_ref, wg_ref, wu_ref, wd_ref, ys_ref, xb_sc):
    @pl.when(pl.program_id(1) == 0)
    def _():
        xb_sc[...] = xs_ref[...].astype(_BF16)
        ys_ref[...] = jnp.zeros_like(ys_ref)

    @pl.when(pl.program_id(0) < nused_ref[0])
    def _():
        ys_ref[...] += _swiglu_acc(xb_sc[...], wg_ref[...], wu_ref[...], wd_ref[...])


def _moe_ffn(xs, tile_expert, n_used, wg, wu, wd, tm):
    rows, d = xs.shape
    f = wg.shape[2]
    tf = 256
    grid_spec = pltpu.PrefetchScalarGridSpec(
        num_scalar_prefetch=2,
        grid=(rows // tm, f // tf),
        in_specs=[
            pl.BlockSpec((tm, d), lambda r, j, te, nu: (r, 0)),
            pl.BlockSpec((None, d, tf), lambda r, j, te, nu: (te[r], 0, j)),
            pl.BlockSpec((None, d, tf), lambda r, j, te, nu: (te[r], 0, j)),
            pl.BlockSpec((None, tf, d), lambda r, j, te, nu: (te[r], j, 0)),
        ],
        out_specs=pl.BlockSpec((tm, d), lambda r, j, te, nu: (r, 0)),
        scratch_shapes=[pltpu.VMEM((tm, d), _BF16)],
    )
    return pl.pallas_call(
        _moe_ffn_kernel,
        grid_spec=grid_spec,
        out_shape=jax.ShapeDtypeStruct((rows, d), _F32),
        compiler_params=_cparams("parallel", "arbitrary"),
        name="moe_expert_ffn",
    )(tile_expert, n_used, xs, wg, wu, wd)


def _combine_kernel(pos_ref, route_ref, x_ref, g_ref, ys_ref, o_ref, buf, sem):
    tc = x_ref.shape[0]

    def start(i, carry):
        for k in range(TOP_K):
            pltpu.make_async_copy(ys_ref.at[pl.ds(pos_ref[0, 0, TOP_K * i + k], 1), :],
                                  buf.at[k, pl.ds(i, 1), :], sem).start()
        return carry

    lax.fori_loop(0, tc, start, 0)

    def wait(i, carry):
        for k in range(TOP_K):
            pltpu.make_async_copy(ys_ref.at[pl.ds(0, 1), :], buf.at[k, pl.ds(0, 1), :], sem).wait()
        return carry

    lax.fori_loop(0, tc, wait, 0)
    route = route_ref[...]
    y = x_ref[...] + route[:, 2:3] * buf[0] + route[:, 3:4] * buf[1]
    o_ref[...] = _rms(y, g_ref[...])


def _combine(pos3, route, x, g, ys):
    n, d = x.shape
    n_chunks, _, width = pos3.shape
    tc = width // TOP_K
    return pl.pallas_call(
        _combine_kernel,
        grid=(n_chunks,),
        in_specs=[
            pl.BlockSpec((1, 1, width), lambda c: (c, 0, 0), memory_space=pltpu.SMEM),
            pl.BlockSpec((tc, LANES), lambda c: (c, 0)),
            pl.BlockSpec((tc, d), lambda c: (c, 0)),
            pl.BlockSpec((1, d), lambda c: (0, 0)),
            pl.BlockSpec(memory_space=pl.ANY),
        ],
        out_specs=pl.BlockSpec((tc, d), lambda c: (c, 0)),
        out_shape=jax.ShapeDtypeStruct((n, d), _F32),
        scratch_shapes=[pltpu.VMEM((TOP_K, tc, d), _F32), pltpu.SemaphoreType.DMA(())],
        compiler_params=_cparams("arbitrary"),
        name="moe_combine",
    )(pos3, route, x, g, ys)


def _rope_tables(pos, dim):
    inv = ROPE_THETA ** (-jnp.arange(0, dim, 2, dtype=_F32) / dim)
    ang = pos.astype(_F32)[:, None] * inv[None, :]
    c, s = jnp.cos(ang), jnp.sin(ang)
    return jnp.concatenate([c, c], axis=-1), jnp.concatenate([-s, s], axis=-1)


def _t5_bucket(rel):
    nb = REL_BUCKETS // 2
    max_exact = nb // 2
    ret = jnp.where(rel > 0, nb, 0)
    n = jnp.abs(rel)
    n_f = jnp.maximum(n, 1).astype(_F32)
    large = max_exact + (jnp.log(n_f / max_exact) / math.log(REL_MAX_DIST / max_exact)
                         * (nb - max_exact)).astype(jnp.int32)
    large = jnp.minimum(large, nb - 1)
    return ret + jnp.where(n < max_exact, n, large)


def _bias_tiles(rel_bias, t):
    assert t >= REL_MAX_DIST
    d = jnp.arange(-2, 3)[:, None, None]
    rel = d * t + jnp.arange(t)[None, None, :] - jnp.arange(t)[None, :, None]
    return jnp.transpose(rel_bias[_t5_bucket(rel)].astype(_F32), (3, 0, 1, 2))


def _moe_plan(experts, tm, n_tiles):
    e_flat = experts.reshape(-1)
    onehot = (e_flat[:, None] == jnp.arange(N_EXPERTS)[None, :]).astype(jnp.int32)
    csum = jnp.cumsum(onehot, axis=0)
    counts = csum[-1]
    rank = jnp.sum(onehot * (csum - 1), axis=1)
    padded = ((counts + tm - 1) // tm) * tm
    ends = jnp.cumsum(padded)
    starts = ends - padded
    pos = starts[e_flat] + rank
    tile_start = jnp.arange(n_tiles, dtype=jnp.int32) * tm
    tile_expert = jnp.minimum(jnp.sum(tile_start[:, None] >= ends[None, :], axis=1), N_EXPERTS - 1)
    n_used = (ends[-1] // tm).reshape(1)
    return pos.astype(jnp.int32), tile_expert.astype(jnp.int32), n_used.astype(jnp.int32)


def kernel(x_prompt, x_sample, rel_bias, norm_mix0, w_in0, pool_w, pool_scale, q_a_norm, w_q_up, kv_a_norm, w_kv_up, w_out0, norm_ffn0, ffn_w_gate, ffn_w_up, ffn_w_down, norm_mix1, w_in1, lambda_q1, lambda_k1, lambda_q2, lambda_k2, diff_subln, gqa_q_norm, gqa_k_norm, w_out1, norm_ffn1, router_w, router_b, moe_w_gate, moe_w_up, moe_w_down, final_norm):
    d = D_MODEL
    bp, sp, _ = x_prompt.shape
    bs, ss, _ = x_sample.shape
    n_p, n_s = bp * sp, bs * ss
    n = n_p + n_s
    groups = [(0, bp, sp), (n_p, bs, ss)]
    x = jnp.concatenate([x_prompt.reshape(n_p, d), x_sample.reshape(n_s, d)], axis=0)
    pos = jnp.concatenate([jnp.tile(jnp.arange(sp), bp), jnp.tile(jnp.arange(ss), bs)])
    row = lambda v: v.reshape(1, -1).astype(_F32)

    w_in0_p = jnp.pad(w_in0[0], ((0, 0), (0, IN0_PAD - IN0_WIDTH))).astype(_BF16)
    u0 = _rms_matmul(x, row(norm_mix0[0]), w_in0_p, tn=640)

    t_pool = _tile(math.gcd(sp, ss), 512)
    tile_rows = jnp.arange(n // t_pool) * t_pool
    tile_pos0 = pos[tile_rows].astype(jnp.int32)
    tile_slen = jnp.where(tile_rows < n_p, sp, ss).astype(jnp.int32)
    a = _pool_mixer(u0, pool_w[0].astype(_BF16), row(pool_scale[0]), tile_pos0, tile_slen, t_pool)

    wq = w_q_up[0].reshape(MLA_Q_LORA, MLA_HEADS, MLA_NOPE + MLA_ROPE)
    wq = jnp.concatenate([wq[:, :, :MLA_NOPE].reshape(MLA_Q_LORA, -1),
                          wq[:, :, MLA_NOPE:].reshape(MLA_Q_LORA, -1)], axis=1).astype(_BF16)
    wkv = w_kv_up[0].reshape(MLA_KV_LORA, MLA_HEADS, MLA_NOPE + MLA_V)
    wkv = jnp.concatenate([wkv[:, :, :MLA_NOPE].reshape(MLA_KV_LORA, -1),
                           wkv[:, :, MLA_NOPE:].reshape(MLA_KV_LORA, -1)], axis=1).astype(_BF16)
    cos_m, sin_m = _rope_tables(pos, MLA_ROPE)
    cos_m = jnp.concatenate([cos_m, cos_m], axis=-1)
    sin_m = jnp.concatenate([sin_m, sin_m], axis=-1)
    q0, k0, v0 = _mla_prep(u0, row(q_a_norm[0]), row(kv_a_norm[0]), wq, wkv, cos_m, sin_m)
    b = None
    for gi, (row0, n_seq, seq) in enumerate(groups):
        b = _flash(q0, k0, v0, b, row0=row0, n_seq=n_seq, seq=seq, heads=MLA_HEADS, kv_group=1,
                   dq=MLA_QK_PAD, dv=MLA_V, q_col0=0, k_col0=0, v_col0=0, t=_tile(seq, 512),
                   name=f"mla_attn_{gi}")
    x = _out_proj(a, b, w_out0[0].astype(_BF16), x)
    x = _ffn(x, row(norm_ffn0[0]), ffn_w_gate[0].astype(_BF16), ffn_w_up[0].astype(_BF16),
             ffn_w_down[0].astype(_BF16))

    lambda_init = 0.8 - 0.6 * math.exp(-0.3 * 1)
    cw = DIFF_HEADS * 2 * DIFF_HD
    col_scale = jnp.concatenate([jnp.full((cw,), DIFF_HD ** -0.5, _F32),
                                 jnp.ones((w_in1.shape[2] - cw,), _F32)])
    u1 = _rms_matmul(x, row(norm_mix1[0]), (w_in1[0] * col_scale[None, :]).astype(_BF16), tn=512)

    c = None
    for gi, (row0, n_seq, seq) in enumerate(groups):
        t = _tile(seq, 512)
        diff_args = (_bias_tiles(rel_bias, t), row(lambda_q1[0]), row(lambda_k1[0]), row(lambda_q2[0]),
                     row(lambda_k2[0]), row(diff_subln[0]))
        c = _flash(u1, u1, u1, c, row0=row0, n_seq=n_seq, seq=seq, heads=DIFF_HEADS, kv_group=1,
                   dq=2 * DIFF_HD, dv=2 * DIFF_HD, q_col0=0, k_col0=DIFF_HEADS, v_col0=2 * DIFF_HEADS,
                   t=t, diff_args=diff_args, lambda_init=lambda_init, name=f"diff_attn_{gi}")

    cr, sr = _rope_tables(pos // GRID_W, AXIAL_DIM)
    cc, sc = _rope_tables(pos % GRID_W, AXIAL_DIM)
    cos_a = jnp.concatenate([cr, cc], axis=-1)
    sin_a = jnp.concatenate([sr, sc], axis=-1)
    qg, kg = _gqa_prep(u1, row(gqa_q_norm[0]), row(gqa_k_norm[0]), cos_a, sin_a)
    v_col0 = (3 * cw + (GQA_HEADS + GQA_KV) * GQA_HD) // GQA_HD
    dd = None
    for gi, (row0, n_seq, seq) in enumerate(groups):
        dd = _flash(qg, kg, u1, dd, row0=row0, n_seq=n_seq, seq=seq, heads=GQA_HEADS, kv_group=GQA_GROUP,
                    dq=GQA_HD, dv=GQA_HD, q_col0=0, k_col0=0, v_col0=v_col0, t=_tile(seq, 512),
                    name=f"gqa_attn_{gi}")
    x = _out_proj(c, dd, w_out1[0].astype(_BF16), x)

    w_r = jnp.pad(router_w[0], ((0, 0), (0, LANES - N_EXPERTS)))
    b_r = jnp.pad(router_b[0], (0, LANES - N_EXPERTS)).reshape(1, LANES)
    h2, route = _router(x, row(norm_ffn1[0]), w_r, b_r)
    experts = route[:, :TOP_K].astype(jnp.int32)
    tm_moe = _tile(n * TOP_K, 1024)
    n_tiles = n * TOP_K // tm_moe + N_EXPERTS
    slot, tile_expert, n_used = _moe_plan(experts, tm_moe, n_tiles)
    tc = _tile(n, 256)
    pos3 = slot.reshape(n // tc, 1, tc * TOP_K)
    xs = _dispatch(h2, pos3, jnp.zeros((n_tiles * tm_moe, d), _F32))
    ys = _moe_ffn(xs, tile_expert, n_used, moe_w_gate[0].astype(_BF16), moe_w_up[0].astype(_BF16),
                  moe_w_down[0].astype(_BF16), tm_moe)
    y = _combine(pos3, route, x, row(final_norm), ys)
    return (y[:n_p].reshape(bp, sp, d), y[n_p:].reshape(bs, ss, d))
```

```python
import functools
import math

import jax
import jax.numpy as jnp
from jax import lax
from jax.experimental import pallas as pl
from jax.experimental.pallas import tpu as pltpu

D_MODEL = 2048
RMS_EPS = 1e-6
ROPE_THETA = 10000.0
GRID_W = 64
POOL_WINDOWS = (2, 4, 8, 16)
POOL_GROUP = D_MODEL // 8
POOL_WIDTH = POOL_GROUP * len(POOL_WINDOWS)
MLA_HEADS = 8
MLA_Q_LORA = D_MODEL // 4
MLA_KV_LORA = D_MODEL // 8
MLA_NOPE = 128
MLA_ROPE = 64
MLA_V = 128
MLA_QK_PAD = 256
IN0_WIDTH = POOL_WIDTH + MLA_Q_LORA + MLA_KV_LORA + MLA_ROPE
IN0_PAD = 1920
DIFF_HEADS = 8
DIFF_HD = 64
GQA_HEADS = 8
GQA_KV = 2
GQA_GROUP = GQA_HEADS // GQA_KV
GQA_HD = 128
AXIAL_DIM = GQA_HD // 2
REL_BUCKETS = 32
REL_MAX_DIST = 128
N_EXPERTS = 8
TOP_K = 2
LANES = 128
HALO = 16
FLASH_SUB = 256
FLASH_TQ = 1024
FLASH_TK = 1024
DIFF_TQ = 512
LOG2E = math.log2(math.e)

_F32 = jnp.float32
_BF16 = jnp.bfloat16
_VMEM_LIMIT = 56 * 1024 * 1024
_NEG = -1e30


def _cparams(*sem):
    return pltpu.CompilerParams(dimension_semantics=sem, vmem_limit_bytes=_VMEM_LIMIT)


def _tile(n, pref):
    t = pref
    while n % t:
        t //= 2
    return t


def _rms(x, g):
    ms = jnp.mean(x * x, axis=-1, keepdims=True)
    return x * lax.rsqrt(ms + RMS_EPS) * g


def _rope_partner(x):
    lane = lax.broadcasted_iota(jnp.int32, x.shape, 1)
    first = (lane % 64) < 32
    return jnp.where(first, pltpu.roll(x, 96, 1), pltpu.roll(x, 32, 1))


def _rms_matmul_kernel(x_ref, g_ref, w_ref, o_ref, h_sc):
    @pl.when(pl.program_id(1) == 0)
    def _():
        h_sc[...] = _rms(x_ref[...], g_ref[...]).astype(_BF16)

    o_ref[...] = jnp.dot(h_sc[...], w_ref[...], preferred_element_type=_F32).astype(o_ref.dtype)


def _rms_matmul(x, g, w, tn):
    n, d = x.shape
    width = w.shape[1]
    tm = _tile(n, 1024)
    return pl.pallas_call(
        _rms_matmul_kernel,
        grid=(n // tm, width // tn),
        in_specs=[
            pl.BlockSpec((tm, d), lambda i, j: (i, 0)),
            pl.BlockSpec((1, d), lambda i, j: (0, 0)),
            pl.BlockSpec((d, tn), lambda i, j: (0, j)),
        ],
        out_specs=pl.BlockSpec((tm, tn), lambda i, j: (i, j)),
        out_shape=jax.ShapeDtypeStruct((n, width), _BF16),
        scratch_shapes=[pltpu.VMEM((tm, d), _BF16)],
        compiler_params=_cparams("parallel", "arbitrary"),
        name="rms_matmul",
    )(x, g, w)


def _out_proj_kernel(a_ref, b_ref, w_ref, x_ref, o_ref):
    ka = a_ref.shape[1]
    acc = jnp.dot(a_ref[...], w_ref[:ka, :], preferred_element_type=_F32)
    acc = acc + jnp.dot(b_ref[...], w_ref[ka:, :], preferred_element_type=_F32)
    o_ref[...] = x_ref[...] + acc


def _out_proj(a, b, w, x):
    n, d = x.shape
    tm = _tile(n, 1024)
    tn = 512
    return pl.pallas_call(
        _out_proj_kernel,
        grid=(n // tm, d // tn),
        in_specs=[
            pl.BlockSpec((tm, a.shape[1]), lambda i, j: (i, 0)),
            pl.BlockSpec((tm, b.shape[1]), lambda i, j: (i, 0)),
            pl.BlockSpec((w.shape[0], tn), lambda i, j: (0, j)),
            pl.BlockSpec((tm, tn), lambda i, j: (i, j)),
        ],
        out_specs=pl.BlockSpec((tm, tn), lambda i, j: (i, j)),
        out_shape=jax.ShapeDtypeStruct((n, d), _F32),
        compiler_params=_cparams("parallel", "arbitrary"),
        name="out_proj",
    )(a, b, w, x)


def _swiglu_acc(h, wg, wu, wd):
    gate = jnp.dot(h, wg, preferred_element_type=_F32)
    up = jnp.dot(h, wu, preferred_element_type=_F32)
    act = (gate / (1.0 + jnp.exp(-gate)) * up).astype(_BF16)
    return jnp.dot(act, wd, preferred_element_type=_F32)


def _ffn_kernel(x_ref, g_ref, wg_ref, wu_ref, wd_ref, o_ref, h_sc):
    @pl.when(pl.program_id(1) == 0)
    def _():
        x = x_ref[...]
        h_sc[...] = _rms(x, g_ref[...]).astype(_BF16)
        o_ref[...] = x

    o_ref[...] += _swiglu_acc(h_sc[...], wg_ref[...], wu_ref[...], wd_ref[...])


def _ffn(x, g, wg, wu, wd):
    n, d = x.shape
    f = wg.shape[1]
    tm = _tile(n, 512)
    tf = 512
    return pl.pallas_call(
        _ffn_kernel,
        grid=(n // tm, f // tf),
        in_specs=[
            pl.BlockSpec((tm, d), lambda i, j: (i, 0)),
            pl.BlockSpec((1, d), lambda i, j: (0, 0)),
            pl.BlockSpec((d, tf), lambda i, j: (0, j)),
            pl.BlockSpec((d, tf), lambda i, j: (0, j)),
            pl.BlockSpec((tf, d), lambda i, j: (j, 0)),
        ],
        out_specs=pl.BlockSpec((tm, d), lambda i, j: (i, 0)),
        out_shape=jax.ShapeDtypeStruct((n, d), _F32),
        scratch_shapes=[pltpu.VMEM((tm, d), _BF16)],
        compiler_params=_cparams("parallel", "arbitrary"),
        name="ffn_swiglu",
    )(x, g, wg, wu, wd)


def _pool_kernel(pos0_ref, slen_ref, prev_ref, cur_ref, next_ref, w_ref, scale_ref, o_ref, buf_sc):
    i = pl.program_id(0)
    t = cur_ref.shape[0]
    pos0 = pos0_ref[i]
    slen = slen_ref[i]
    cur = cur_ref[...].astype(_F32)
    buf_sc[0:HALO, :] = jnp.where(pos0 > 0, prev_ref[...].astype(_F32), 0.0)
    buf_sc[HALO:HALO + t, :] = cur
    buf_sc[HALO + t:, :] = jnp.where(pos0 + t < slen, next_ref[...].astype(_F32), 0.0)
    tpos = pos0 + lax.broadcasted_iota(jnp.int32, (t, 1), 0)
    for g, win in enumerate(POOL_WINDOWS):
        c0, c1 = g * POOL_GROUP, (g + 1) * POOL_GROUP
        half = win // 2
        acc = buf_sc[HALO - half:HALO - half + t, c0:c1]
        for dlt in range(-half + 1, half):
            acc = acc + buf_sc[HALO + dlt:HALO + dlt + t, c0:c1]
        count = (jnp.minimum(tpos + half, slen) - jnp.maximum(tpos - half, 0)).astype(_F32)
        p = (acc / count - cur[:, c0:c1]).astype(_BF16)
        y = jnp.dot(p, w_ref[g], preferred_element_type=_F32)
        o_ref[:, c0:c1] = (y * scale_ref[:, c0:c1]).astype(o_ref.dtype)


def _pool_mixer(u, pool_w, pool_scale, tile_pos0, tile_slen, t):
    n = u.shape[0]
    hb = t // HALO
    last = n // HALO - 1
    grid_spec = pltpu.PrefetchScalarGridSpec(
        num_scalar_prefetch=2,
        grid=(n // t,),
        in_specs=[
            pl.BlockSpec((HALO, POOL_WIDTH), lambda i, p, s: (jnp.maximum(i * hb - 1, 0), 0)),
            pl.BlockSpec((t, POOL_WIDTH), lambda i, p, s: (i, 0)),
            pl.BlockSpec((HALO, POOL_WIDTH), lambda i, p, s: (jnp.minimum((i + 1) * hb, last), 0)),
            pl.BlockSpec((len(POOL_WINDOWS), POOL_GROUP, POOL_GROUP), lambda i, p, s: (0, 0, 0)),
            pl.BlockSpec((1, POOL_WIDTH), lambda i, p, s: (0, 0)),
        ],
        out_specs=pl.BlockSpec((t, POOL_WIDTH), lambda i, p, s: (i, 0)),
        scratch_shapes=[pltpu.VMEM((t + 2 * HALO, POOL_WIDTH), _F32)],
    )
    return pl.pallas_call(
        _pool_kernel,
        grid_spec=grid_spec,
        out_shape=jax.ShapeDtypeStruct((n, POOL_WIDTH), _BF16),
        compiler_params=_cparams("arbitrary"),
        name="pool_mixer",
    )(tile_pos0, tile_slen, u, u, u, pool_w, pool_scale)


def _mla_prep_kernel(ql_ref, kvl_ref, kr_ref, gq_ref, gkv_ref, wq_ref, wkv_ref, cos_ref, sin_ref,
                     q_ref, k_ref, v_ref):
    scale = (MLA_NOPE + MLA_ROPE) ** -0.5 * LOG2E
    cos = cos_ref[...]
    sin = sin_ref[...]
    lane = lax.broadcasted_iota(jnp.int32, cos.shape, 1)
    low = lane < MLA_ROPE

    qn = _rms(ql_ref[...].astype(_F32), gq_ref[...]).astype(_BF16)
    q = jnp.dot(qn, wq_ref[...], preferred_element_type=_F32)
    kvn = _rms(kvl_ref[...].astype(_F32), gkv_ref[...]).astype(_BF16)
    kv = jnp.dot(kvn, wkv_ref[...], preferred_element_type=_F32)
    nope_w = MLA_HEADS * MLA_NOPE
    v_ref[...] = kv[:, nope_w:].astype(v_ref.dtype)

    kr = kr_ref[...].astype(_F32)
    k_pe = jnp.where(low, kr * cos + _rope_partner(kr) * sin, 0.0).astype(k_ref.dtype)
    for pair in range(MLA_HEADS // 2):
        pe = q[:, nope_w + pair * LANES:nope_w + (pair + 1) * LANES]
        roped = (pe * cos + _rope_partner(pe) * sin) * scale
        for sub in range(2):
            h = 2 * pair + sub
            pe_h = roped if sub == 0 else pltpu.roll(roped, MLA_ROPE, 1)
            base = h * MLA_QK_PAD
            q_ref[:, base:base + MLA_NOPE] = (q[:, h * MLA_NOPE:(h + 1) * MLA_NOPE] * scale).astype(q_ref.dtype)
            q_ref[:, base + MLA_NOPE:base + MLA_QK_PAD] = jnp.where(low, pe_h, 0.0).astype(q_ref.dtype)
            k_ref[:, base:base + MLA_NOPE] = kv[:, h * MLA_NOPE:(h + 1) * MLA_NOPE].astype(k_ref.dtype)
            k_ref[:, base + MLA_NOPE:base + MLA_QK_PAD] = k_pe


def _mla_prep(u, gq, gkv, wq, wkv, cos_t, sin_t):
    n = u.shape[0]
    tm = _tile(n, 512)
    qk_w = MLA_HEADS * MLA_QK_PAD
    v_w = MLA_HEADS * MLA_V
    return pl.pallas_call(
        _mla_prep_kernel,
        grid=(n // tm,),
        in_specs=[
            pl.BlockSpec((tm, MLA_Q_LORA), lambda i: (i, POOL_WIDTH // MLA_Q_LORA)),
            pl.BlockSpec((tm, MLA_KV_LORA), lambda i: (i, (POOL_WIDTH + MLA_Q_LORA) // MLA_KV_LORA)),
            pl.BlockSpec((tm, LANES), lambda i: (i, (POOL_WIDTH + MLA_Q_LORA + MLA_KV_LORA) // LANES)),
            pl.BlockSpec((1, MLA_Q_LORA), lambda i: (0, 0)),
            pl.BlockSpec((1, MLA_KV_LORA), lambda i: (0, 0)),
            pl.BlockSpec(wq.shape, lambda i: (0, 0)),
            pl.BlockSpec(wkv.shape, lambda i: (0, 0)),
            pl.BlockSpec((tm, LANES), lambda i: (i, 0)),
            pl.BlockSpec((tm, LANES), lambda i: (i, 0)),
        ],
        out_specs=[
            pl.BlockSpec((tm, qk_w), lambda i: (i, 0)),
            pl.BlockSpec((tm, qk_w), lambda i: (i, 0)),
            pl.BlockSpec((tm, v_w), lambda i: (i, 0)),
        ],
        out_shape=[
            jax.ShapeDtypeStruct((n, qk_w), _BF16),
            jax.ShapeDtypeStruct((n, qk_w), _BF16),
            jax.ShapeDtypeStruct((n, v_w), _BF16),
        ],
        compiler_params=_cparams("parallel"),
        name="mla_prep",
    )(u, u, u, gq, gkv, wq, wkv, cos_t, sin_t)


def _gqa_prep_kernel(dq_ref, dk_ref, gq_ref, gk_ref, cos_ref, sin_ref, q_ref, k_ref):
    scale = GQA_HD ** -0.5 * LOG2E
    cos = cos_ref[...]
    sin = sin_ref[...]

    def norm_rope(x, g):
        xn = _rms(x.astype(_F32), g)
        return xn * cos + _rope_partner(xn) * sin

    for h in range(GQA_HEADS):
        sl = slice(h * GQA_HD, (h + 1) * GQA_HD)
        q_ref[:, sl] = (norm_rope(dq_ref[:, sl], gq_ref[...]) * scale).astype(q_ref.dtype)
    for h in range(GQA_KV):
        sl = slice(h * GQA_HD, (h + 1) * GQA_HD)
        k_ref[:, sl] = norm_rope(dk_ref[:, sl], gk_ref[...]).astype(k_ref.dtype)


def _gqa_prep(u, gq, gk, cos_t, sin_t):
    n = u.shape[0]
    tm = _tile(n, 512)
    q_w = GQA_HEADS * GQA_HD
    k_w = GQA_KV * GQA_HD
    dq_off = 3 * DIFF_HEADS * 2 * DIFF_HD
    return pl.pallas_call(
        _gqa_prep_kernel,
        grid=(n // tm,),
        in_specs=[
            pl.BlockSpec((tm, q_w), lambda i: (i, dq_off // q_w)),
            pl.BlockSpec((tm, k_w), lambda i: (i, (dq_off + q_w) // k_w)),
            pl.BlockSpec((1, GQA_HD), lambda i: (0, 0)),
            pl.BlockSpec((1, GQA_HD), lambda i: (0, 0)),
            pl.BlockSpec((tm, LANES), lambda i: (i, 0)),
            pl.BlockSpec((tm, LANES), lambda i: (i, 0)),
        ],
        out_specs=[
            pl.BlockSpec((tm, q_w), lambda i: (i, 0)),
            pl.BlockSpec((tm, k_w), lambda i: (i, 0)),
        ],
        out_shape=[
            jax.ShapeDtypeStruct((n, q_w), _BF16),
            jax.ShapeDtypeStruct((n, k_w), _BF16),
        ],
        compiler_params=_cparams("parallel"),
        name="gqa_prep",
    )(u, u, gq, gk, cos_t, sin_t)


def _flash_step(q_src, k_ref, v_ref, bias_ref, shift, m_sc, l_sc, acc_sc, tq):
    k = k_ref[...]
    v = v_ref[...]
    n_chunks = k.shape[0] // LANES
    starts = range(0, m_sc.shape[0], FLASH_SUB)
    scores = [lax.dot_general(q_src[r0:r0 + FLASH_SUB, :], k, (((1,), (1,)), ((), ())),
                              preferred_element_type=_F32) for r0 in starts]
    for r0, s in zip(starts, scores):
        sl = slice(r0, r0 + FLASH_SUB)
        chunks = [s[:, c * LANES:(c + 1) * LANES] for c in range(n_chunks)]
        if bias_ref is not None:
            b0 = r0 % tq
            chunks = [bias_ref[b0:b0 + FLASH_SUB, c * LANES:(c + 1) * LANES] + ch
                      for c, ch in enumerate(chunks)]
        m_prev = m_sc[sl, :]
        row_max = jnp.max(functools.reduce(jnp.maximum, chunks), axis=-1, keepdims=True)
        if shift is not None:
            row_max = row_max + shift
        m_new = jnp.maximum(m_prev, row_max)
        alpha = jnp.exp2(m_prev - m_new)
        sub = m_new if shift is None else m_new - shift
        ps = [jnp.exp2(ch - sub) for ch in chunks]
        l_sc[sl, :] = alpha * l_sc[sl, :] + functools.reduce(jnp.add, ps)
        p = jnp.concatenate(ps, axis=1).astype(_BF16)
        acc_sc[sl, :] = alpha * acc_sc[sl, :] + jnp.dot(p, v, preferred_element_type=_F32)
        m_sc[sl, :] = m_new


def _flash_kernel(*refs, diff, lambda_init):
    if diff:
        (q_ref, k_ref, v_ref, bias_ref, far_ref, lq1_ref, lk1_ref, lq2_ref, lk2_ref, sub_ref, _, o_ref,
         q_sc, m_sc, l_sc, acc_sc) = refs
    else:
        q_ref, k_ref, v_ref, _, o_ref, m_sc, l_sc, acc_sc = refs
    ki = pl.program_id(3)
    tq = q_ref.shape[0]

    @pl.when(ki == 0)
    def _():
        m_sc[...] = jnp.full_like(m_sc, _NEG)
        l_sc[...] = jnp.zeros_like(l_sc)
        acc_sc[...] = jnp.zeros_like(acc_sc)
        if diff:
            q = q_ref[...]
            lane = lax.broadcasted_iota(jnp.int32, q.shape, 1)
            zero = jnp.zeros_like(q)
            q_sc[0:tq, :] = jnp.where(lane < DIFF_HD, q, zero)
            q_sc[tq:, :] = jnp.where(lane >= DIFF_HD, q, zero)

    if diff:
        h = pl.program_id(1)
        ratio = k_ref.shape[0] // tq
        off = ratio * ki - pl.program_id(2)
        near = jnp.logical_and(off >= -ratio, off <= 1)

        @pl.when(near)
        def _():
            _flash_step(q_sc, k_ref, v_ref, bias_ref, None, m_sc, l_sc, acc_sc, tq)

        @pl.when(jnp.logical_not(near))
        def _():
            shift = jnp.where(off > 1, far_ref[h, 1], far_ref[h, 0])
            _flash_step(q_sc, k_ref, v_ref, None, shift, m_sc, l_sc, acc_sc, tq)
    else:
        _flash_step(q_ref, k_ref, v_ref, None, None, m_sc, l_sc, acc_sc, tq)

    @pl.when(ki == pl.num_programs(3) - 1)
    def _():
        o = acc_sc[...] / jnp.sum(l_sc[...], axis=-1, keepdims=True)
        if diff:
            lam = (jnp.exp(jnp.sum(lq1_ref[...] * lk1_ref[...], axis=-1, keepdims=True))
                   - jnp.exp(jnp.sum(lq2_ref[...] * lk2_ref[...], axis=-1, keepdims=True))
                   + lambda_init)
            o = o[0:tq, :] - lam * o[tq:, :]
            o = _rms(o, sub_ref[...]) * (1.0 - lambda_init)
        o_ref[...] = o.astype(o_ref.dtype)


def _flash(q_arr, k_arr, v_arr, prev_out, *, row0, n_seq, seq, heads, kv_group, dq, dv,
           q_col0, k_col0, v_col0, tq, tk, diff_args=None, lambda_init=0.0, name):
    n = q_arr.shape[0]
    nq, nk = seq // tq, seq // tk
    diff = diff_args is not None
    rows = 2 * tq if diff else tq
    assert rows % FLASH_SUB == 0 and tq % FLASH_SUB == 0

    def q_map(b, h, qi, ki):
        return (row0 // tq + b * nq + qi, q_col0 + h)

    def k_map(b, h, qi, ki):
        return (row0 // tk + b * nk + ki, k_col0 + h // kv_group)

    def v_map(b, h, qi, ki):
        return (row0 // tk + b * nk + ki, v_col0 + h // kv_group)

    def o_map(b, h, qi, ki):
        return (row0 // tq + b * nq + qi, h)

    in_specs = [pl.BlockSpec((tq, dq), q_map), pl.BlockSpec((tk, dq), k_map), pl.BlockSpec((tk, dv), v_map)]
    args = [q_arr, k_arr, v_arr]
    scratch = []
    if diff:
        assert tk % tq == 0 and tq >= REL_MAX_DIST
        ratio = tk // tq
        bias, far, lq1, lk1, lq2, lk2, sub = diff_args
        in_specs.append(pl.BlockSpec(
            (None, None, tq, tk),
            lambda b, h, qi, ki: (h, jnp.clip(ratio * ki - qi, -ratio, 1) + ratio, 0, 0)))
        in_specs.append(pl.BlockSpec(memory_space=pltpu.SMEM))
        in_specs += [pl.BlockSpec((1, DIFF_HD), lambda b, h, qi, ki: (0, 0))] * 4
        in_specs.append(pl.BlockSpec((1, 2 * DIFF_HD), lambda b, h, qi, ki: (0, 0)))
        args += [bias, far, lq1, lk1, lq2, lk2, sub]
        scratch.append(pltpu.VMEM((rows, dq), _BF16))
    out_shape = jax.ShapeDtypeStruct((n, heads * dv), _BF16)
    if prev_out is None:
        prev_out = jnp.zeros(out_shape.shape, out_shape.dtype)
    aliases = {len(args): 0}
    in_specs.append(pl.BlockSpec(memory_space=pl.ANY))
    args.append(prev_out)
    assert dv == LANES
    scratch += [pltpu.VMEM((rows, LANES), _F32), pltpu.VMEM((rows, LANES), _F32), pltpu.VMEM((rows, dv), _F32)]
    return pl.pallas_call(
        functools.partial(_flash_kernel, diff=diff, lambda_init=lambda_init),
        grid=(n_seq, heads, nq, nk),
        in_specs=in_specs,
        out_specs=pl.BlockSpec((tq, dv), o_map),
        out_shape=out_shape,
        scratch_shapes=scratch,
        input_output_aliases=aliases,
        compiler_params=_cparams("parallel", "parallel", "parallel", "arbitrary"),
        name=name,
    )(*args)


def _router_kernel(x_ref, g_ref, w_ref, b_ref, h_ref, route_ref):
    h = _rms(x_ref[...], g_ref[...])
    h_ref[...] = h
    logits = jnp.dot(h, w_ref[...], preferred_element_type=_F32, precision=lax.Precision.HIGHEST) + b_ref[...]
    lane = lax.broadcasted_iota(jnp.int32, logits.shape, 1)
    logits = jnp.where(lane < N_EXPERTS, logits, _NEG)
    v1 = jnp.max(logits, axis=-1, keepdims=True)
    i1 = jnp.min(jnp.where(logits == v1, lane, LANES), axis=-1, keepdims=True)
    rest = jnp.where(lane == i1, _NEG, logits)
    v2 = jnp.max(rest, axis=-1, keepdims=True)
    i2 = jnp.min(jnp.where(rest == v2, lane, LANES), axis=-1, keepdims=True)
    g1 = 1.0 / (1.0 + jnp.exp(v2 - v1))
    g2 = 1.0 - g1
    route = jnp.where(lane == 0, i1.astype(_F32),
                      jnp.where(lane == 1, i2.astype(_F32),
                                jnp.where(lane == 2, g1, jnp.where(lane == 3, g2, 0.0))))
    route_ref[...] = route


def _router(x, g, w_pad, b_pad):
    n, d = x.shape
    tm = _tile(n, 512)
    return pl.pallas_call(
        _router_kernel,
        grid=(n // tm,),
        in_specs=[
            pl.BlockSpec((tm, d), lambda i: (i, 0)),
            pl.BlockSpec((1, d), lambda i: (0, 0)),
            pl.BlockSpec((d, LANES), lambda i: (0, 0)),
            pl.BlockSpec((1, LANES), lambda i: (0, 0)),
        ],
        out_specs=[pl.BlockSpec((tm, d), lambda i: (i, 0)), pl.BlockSpec((tm, LANES), lambda i: (i, 0))],
        out_shape=[jax.ShapeDtypeStruct((n, d), _F32), jax.ShapeDtypeStruct((n, LANES), _F32)],
        compiler_params=_cparams("parallel"),
        name="moe_router",
    )(x, g, w_pad, b_pad)


def _gather_rows(src_hbm, idx_ref, idx0, stride, dst, n_rows, sem, *, wait):
    def step(i, carry):
        row = idx_ref[0, 0, idx0 + stride * i]
        copy = pltpu.make_async_copy(src_hbm.at[pl.ds(row, 1), :], dst.at[pl.ds(i, 1), :], sem)
        if wait:
            copy.wait()
        else:
            copy.start()
        return carry

    lax.fori_loop(0, n_rows, step, 0, unroll=8)


def _moe_ffn_kernel(te_ref, nused_ref, tok_ref, h_ref, wg_ref, wu_ref, wd_ref, ys_ref, xf_sc, xb_sc, sem):
    used = pl.program_id(0) < nused_ref[0]

    @pl.when(pl.program_id(1) == 0)
    def _():
        ys_ref[...] = jnp.zeros_like(ys_ref)

        @pl.when(used)
        def _():
            for wait in (False, True):
                _gather_rows(h_ref, tok_ref, 0, 1, xf_sc, xf_sc.shape[0], sem, wait=wait)
            xb_sc[...] = xf_sc[...].astype(_BF16)

    @pl.when(used)
    def _():
        ys_ref[...] += _swiglu_acc(xb_sc[...], wg_ref[...], wu_ref[...], wd_ref[...])


def _moe_ffn(h, row_token3, tile_expert, n_used, wg, wu, wd):
    n_tiles, _, tm = row_token3.shape
    d = h.shape[1]
    f = wg.shape[2]
    tf = 512
    grid_spec = pltpu.PrefetchScalarGridSpec(
        num_scalar_prefetch=2,
        grid=(n_tiles, f // tf),
        in_specs=[
            pl.BlockSpec((1, 1, tm), lambda r, j, te, nu: (r, 0, 0), memory_space=pltpu.SMEM),
            pl.BlockSpec(memory_space=pl.ANY),
            pl.BlockSpec((None, d, tf), lambda r, j, te, nu: (te[r], 0, j)),
            pl.BlockSpec((None, d, tf), lambda r, j, te, nu: (te[r], 0, j)),
            pl.BlockSpec((None, tf, d), lambda r, j, te, nu: (te[r], j, 0)),
        ],
        out_specs=pl.BlockSpec((tm, d), lambda r, j, te, nu: (r, 0)),
        scratch_shapes=[pltpu.VMEM((tm, d), _F32), pltpu.VMEM((tm, d), _BF16), pltpu.SemaphoreType.DMA(())],
    )
    return pl.pallas_call(
        _moe_ffn_kernel,
        grid_spec=grid_spec,
        out_shape=jax.ShapeDtypeStruct((n_tiles * tm, d), _F32),
        compiler_params=_cparams("arbitrary", "arbitrary"),
        name="moe_expert_ffn",
    )(tile_expert, n_used, row_token3, h, wg, wu, wd)


def _combine_kernel(pos_ref, route_ref, x_ref, g_ref, ys_ref, o_ref, buf, sem):
    tc = x_ref.shape[0]
    for wait in (False, True):
        for k in range(TOP_K):
            _gather_rows(ys_ref, pos_ref, k, TOP_K, buf.at[k], tc, sem.at[k], wait=wait)
    route = route_ref[...]
    y = x_ref[...] + route[:, 2:3] * buf[0] + route[:, 3:4] * buf[1]
    o_ref[...] = _rms(y, g_ref[...])


def _combine(pos3, route, x, g, ys):
    n, d = x.shape
    n_chunks, _, width = pos3.shape
    tc = width // TOP_K
    return pl.pallas_call(
        _combine_kernel,
        grid=(n_chunks,),
        in_specs=[
            pl.BlockSpec((1, 1, width), lambda c: (c, 0, 0), memory_space=pltpu.SMEM),
            pl.BlockSpec((tc, LANES), lambda c: (c, 0)),
            pl.BlockSpec((tc, d), lambda c: (c, 0)),
            pl.BlockSpec((1, d), lambda c: (0, 0)),
            pl.BlockSpec(memory_space=pl.ANY),
        ],
        out_specs=pl.BlockSpec((tc, d), lambda c: (c, 0)),
        out_shape=jax.ShapeDtypeStruct((n, d), _F32),
        scratch_shapes=[pltpu.VMEM((TOP_K, tc, d), _F32), pltpu.SemaphoreType.DMA((TOP_K,))],
        compiler_params=_cparams("arbitrary"),
        name="moe_combine",
    )(pos3, route, x, g, ys)


def _rope_tables(pos, dim):
    inv = ROPE_THETA ** (-jnp.arange(0, dim, 2, dtype=_F32) / dim)
    ang = pos.astype(_F32)[:, None] * inv[None, :]
    c, s = jnp.cos(ang), jnp.sin(ang)
    return jnp.concatenate([c, c], axis=-1), jnp.concatenate([-s, s], axis=-1)


def _t5_bucket(rel):
    nb = REL_BUCKETS // 2
    max_exact = nb // 2
    ret = jnp.where(rel > 0, nb, 0)
    n = jnp.abs(rel)
    n_f = jnp.maximum(n, 1).astype(_F32)
    large = max_exact + (jnp.log(n_f / max_exact) / math.log(REL_MAX_DIST / max_exact)
                         * (nb - max_exact)).astype(jnp.int32)
    large = jnp.minimum(large, nb - 1)
    return ret + jnp.where(n < max_exact, n, large)


def _bias_tiles(rel_bias, tq, tk):
    table = rel_bias.astype(_F32) * LOG2E
    off = jnp.arange(-(tk // tq), 2)[:, None, None]
    rel = off * tq + jnp.arange(tk)[None, None, :] - jnp.arange(tq)[None, :, None]
    onehot = (_t5_bucket(rel)[..., None] == jnp.arange(REL_BUCKETS)).astype(_F32)
    tiles = jnp.einsum("dqkb,bh->hdqk", onehot, table, precision=lax.Precision.HIGHEST)
    far = table[_t5_bucket(jnp.array([-REL_MAX_DIST, REL_MAX_DIST]))].T
    return tiles, far


def _moe_plan(experts, tm, n_tiles):
    e_flat = experts.reshape(-1)
    onehot = (e_flat[:, None] == jnp.arange(N_EXPERTS)[None, :]).astype(jnp.int32)
    csum = jnp.cumsum(onehot, axis=0)
    counts = csum[-1]
    rank = jnp.sum(onehot * (csum - 1), axis=1)
    padded = ((counts + tm - 1) // tm) * tm
    ends = jnp.cumsum(padded)
    starts = ends - padded
    pos = starts[e_flat] + rank
    tile_start = jnp.arange(n_tiles, dtype=jnp.int32) * tm
    tile_expert = jnp.minimum(jnp.sum(tile_start[:, None] >= ends[None, :], axis=1), N_EXPERTS - 1)
    n_used = (ends[-1] // tm).reshape(1)
    pair_token = jnp.arange(e_flat.shape[0], dtype=jnp.int32) // TOP_K
    row_token = jnp.zeros((n_tiles * tm,), jnp.int32).at[pos].set(pair_token, unique_indices=True)
    return pos.astype(jnp.int32), row_token, tile_expert.astype(jnp.int32), n_used.astype(jnp.int32)


def kernel(x_prompt, x_sample, rel_bias, norm_mix0, w_in0, pool_w, pool_scale, q_a_norm, w_q_up, kv_a_norm, w_kv_up, w_out0, norm_ffn0, ffn_w_gate, ffn_w_up, ffn_w_down, norm_mix1, w_in1, lambda_q1, lambda_k1, lambda_q2, lambda_k2, diff_subln, gqa_q_norm, gqa_k_norm, w_out1, norm_ffn1, router_w, router_b, moe_w_gate, moe_w_up, moe_w_down, final_norm):
    d = D_MODEL
    bp, sp, _ = x_prompt.shape
    bs, ss, _ = x_sample.shape
    n_p, n_s = bp * sp, bs * ss
    n = n_p + n_s
    groups = [(0, bp, sp), (n_p, bs, ss)]
    x = jnp.concatenate([x_prompt.reshape(n_p, d), x_sample.reshape(n_s, d)], axis=0)
    pos = jnp.concatenate([jnp.tile(jnp.arange(sp), bp), jnp.tile(jnp.arange(ss), bs)])
    row = lambda v: v.reshape(1, -1).astype(_F32)

    w_in0_p = jnp.pad(w_in0[0], ((0, 0), (0, IN0_PAD - IN0_WIDTH))).astype(_BF16)
    u0 = _rms_matmul(x, row(norm_mix0[0]), w_in0_p, tn=640)

    t_pool = _tile(math.gcd(sp, ss), 512)
    tile_rows = jnp.arange(n // t_pool) * t_pool
    tile_pos0 = pos[tile_rows].astype(jnp.int32)
    tile_slen = jnp.where(tile_rows < n_p, sp, ss).astype(jnp.int32)
    a = _pool_mixer(u0, pool_w[0].astype(_BF16), row(pool_scale[0]), tile_pos0, tile_slen, t_pool)

    wq = w_q_up[0].reshape(MLA_Q_LORA, MLA_HEADS, MLA_NOPE + MLA_ROPE)
    wq = jnp.concatenate([wq[:, :, :MLA_NOPE].reshape(MLA_Q_LORA, -1),
                          wq[:, :, MLA_NOPE:].reshape(MLA_Q_LORA, -1)], axis=1).astype(_BF16)
    wkv = w_kv_up[0].reshape(MLA_KV_LORA, MLA_HEADS, MLA_NOPE + MLA_V)
    wkv = jnp.concatenate([wkv[:, :, :MLA_NOPE].reshape(MLA_KV_LORA, -1),
                           wkv[:, :, MLA_NOPE:].reshape(MLA_KV_LORA, -1)], axis=1).astype(_BF16)
    cos_m, sin_m = _rope_tables(pos, MLA_ROPE)
    cos_m = jnp.concatenate([cos_m, cos_m], axis=-1)
    sin_m = jnp.concatenate([sin_m, sin_m], axis=-1)
    q0, k0, v0 = _mla_prep(u0, row(q_a_norm[0]), row(kv_a_norm[0]), wq, wkv, cos_m, sin_m)
    b = None
    for gi, (row0, n_seq, seq) in enumerate(groups):
        b = _flash(q0, k0, v0, b, row0=row0, n_seq=n_seq, seq=seq, heads=MLA_HEADS, kv_group=1,
                   dq=MLA_QK_PAD, dv=MLA_V, q_col0=0, k_col0=0, v_col0=0, tq=_tile(seq, FLASH_TQ),
                   tk=_tile(seq, FLASH_TK), name=f"mla_attn_{gi}")
    x = _out_proj(a, b, w_out0[0].astype(_BF16), x)
    x = _ffn(x, row(norm_ffn0[0]), ffn_w_gate[0].astype(_BF16), ffn_w_up[0].astype(_BF16),
             ffn_w_down[0].astype(_BF16))

    lambda_init = 0.8 - 0.6 * math.exp(-0.3 * 1)
    cw = DIFF_HEADS * 2 * DIFF_HD
    col_scale = jnp.concatenate([jnp.full((cw,), DIFF_HD ** -0.5 * LOG2E, _F32),
                                 jnp.ones((w_in1.shape[2] - cw,), _F32)])
    u1 = _rms_matmul(x, row(norm_mix1[0]), (w_in1[0] * col_scale[None, :]).astype(_BF16), tn=512)

    c = None
    for gi, (row0, n_seq, seq) in enumerate(groups):
        tq, tk = _tile(seq, DIFF_TQ), _tile(seq, FLASH_TK)
        tiles, far = _bias_tiles(rel_bias, tq, tk)
        diff_args = (tiles, far, row(lambda_q1[0]), row(lambda_k1[0]), row(lambda_q2[0]),
                     row(lambda_k2[0]), row(diff_subln[0]))
        c = _flash(u1, u1, u1, c, row0=row0, n_seq=n_seq, seq=seq, heads=DIFF_HEADS, kv_group=1,
                   dq=2 * DIFF_HD, dv=2 * DIFF_HD, q_col0=0, k_col0=DIFF_HEADS, v_col0=2 * DIFF_HEADS,
                   tq=tq, tk=tk, diff_args=diff_args, lambda_init=lambda_init, name=f"diff_attn_{gi}")

    cr, sr = _rope_tables(pos // GRID_W, AXIAL_DIM)
    cc, sc = _rope_tables(pos % GRID_W, AXIAL_DIM)
    cos_a = jnp.concatenate([cr, cc], axis=-1)
    sin_a = jnp.concatenate([sr, sc], axis=-1)
    qg, kg = _gqa_prep(u1, row(gqa_q_norm[0]), row(gqa_k_norm[0]), cos_a, sin_a)
    v_col0 = (3 * cw + (GQA_HEADS + GQA_KV) * GQA_HD) // GQA_HD
    dd = None
    for gi, (row0, n_seq, seq) in enumerate(groups):
        dd = _flash(qg, kg, u1, dd, row0=row0, n_seq=n_seq, seq=seq, heads=GQA_HEADS, kv_group=GQA_GROUP,
                    dq=GQA_HD, dv=GQA_HD, q_col0=0, k_col0=0, v_col0=v_col0, tq=_tile(seq, FLASH_TQ),
                    tk=_tile(seq, FLASH_TK), name=f"gqa_attn_{gi}")
    x = _out_proj(c, dd, w_out1[0].astype(_BF16), x)

    w_r = jnp.pad(router_w[0], ((0, 0), (0, LANES - N_EXPERTS)))
    b_r = jnp.pad(router_b[0], (0, LANES - N_EXPERTS)).reshape(1, LANES)
    h2, route = _router(x, row(norm_ffn1[0]), w_r, b_r)
    experts = route[:, :TOP_K].astype(jnp.int32)
    tm_moe = _tile(n * TOP_K, 1024)
    n_tiles = n * TOP_K // tm_moe + N_EXPERTS
    slot, row_token, tile_expert, n_used = _moe_plan(experts, tm_moe, n_tiles)
    tc = _tile(n, 256)
    pos3 = slot.reshape(n // tc, 1, tc * TOP_K)
    ys = _moe_ffn(h2, row_token.reshape(n_tiles, 1, tm_moe), tile_expert, n_used,
                  moe_w_gate[0].astype(_BF16), moe_w_up[0].astype(_BF16), moe_w_down[0].astype(_BF16))
    y = _combine(pos3, route, x, row(final_norm), ys)
    return (y[:n_p].reshape(bp, sp, d), y[n_p:].reshape(bs, ss, d))
```

```python
import functools
import math

import jax
import jax.numpy as jnp
from jax import lax
from jax.experimental import pallas as pl
from jax.experimental.pallas import tpu as pltpu

D_MODEL = 2048
RMS_EPS = 1e-6
ROPE_THETA = 10000.0
GRID_W = 64
POOL_WINDOWS = (2, 4, 8, 16)
POOL_GROUP = D_MODEL // 8
POOL_WIDTH = POOL_GROUP * len(POOL_WINDOWS)
MLA_HEADS = 8
MLA_Q_LORA = D_MODEL // 4
MLA_KV_LORA = D_MODEL // 8
MLA_NOPE = 128
MLA_ROPE = 64
MLA_V = 128
MLA_QK_PAD = 256
IN0_WIDTH = POOL_WIDTH + MLA_Q_LORA + MLA_KV_LORA + MLA_ROPE
IN0_PAD = 1920
DIFF_HEADS = 8
DIFF_HD = 64
GQA_HEADS = 8
GQA_KV = 2
GQA_GROUP = GQA_HEADS // GQA_KV
GQA_HD = 128
AXIAL_DIM = GQA_HD // 2
REL_BUCKETS = 32
REL_MAX_DIST = 128
N_EXPERTS = 8
TOP_K = 2
LANES = 128
HALO = 16
FLASH_SUB = 256
FLASH_TQ = 2048
FLASH_TK = 2048
FLASH_KH = 1024
DIFF_TQ = 512
LOG2E = math.log2(math.e)

_F32 = jnp.float32
_BF16 = jnp.bfloat16
_VMEM_LIMIT = 56 * 1024 * 1024
_NEG = -1e30


def _cparams(*sem):
    return pltpu.CompilerParams(dimension_semantics=sem, vmem_limit_bytes=_VMEM_LIMIT)


def _tile(n, pref):
    t = pref
    while n % t:
        t //= 2
    return t


def _rms(x, g):
    ms = jnp.mean(x * x, axis=-1, keepdims=True)
    return x * lax.rsqrt(ms + RMS_EPS) * g


def _rope_partner(x):
    lane = lax.broadcasted_iota(jnp.int32, x.shape, 1)
    first = (lane % 64) < 32
    return jnp.where(first, pltpu.roll(x, 96, 1), pltpu.roll(x, 32, 1))


def _rms_matmul_kernel(x_ref, g_ref, w_ref, o_ref, h_sc):
    @pl.when(pl.program_id(1) == 0)
    def _():
        h_sc[...] = _rms(x_ref[...], g_ref[...]).astype(_BF16)

    o_ref[...] = jnp.dot(h_sc[...], w_ref[...], preferred_element_type=_F32).astype(o_ref.dtype)


def _rms_matmul(x, g, w, tn):
    n, d = x.shape
    width = w.shape[1]
    tm = _tile(n, 1024)
    return pl.pallas_call(
        _rms_matmul_kernel,
        grid=(n // tm, width // tn),
        in_specs=[
            pl.BlockSpec((tm, d), lambda i, j: (i, 0)),
            pl.BlockSpec((1, d), lambda i, j: (0, 0)),
            pl.BlockSpec((d, tn), lambda i, j: (0, j)),
        ],
        out_specs=pl.BlockSpec((tm, tn), lambda i, j: (i, j)),
        out_shape=jax.ShapeDtypeStruct((n, width), _BF16),
        scratch_shapes=[pltpu.VMEM((tm, d), _BF16)],
        compiler_params=_cparams("parallel", "arbitrary"),
        name="rms_matmul",
    )(x, g, w)


def _out_proj_kernel(a_ref, b_ref, w_ref, x_ref, o_ref):
    ka = a_ref.shape[1]
    acc = jnp.dot(a_ref[...], w_ref[:ka, :], preferred_element_type=_F32)
    acc = acc + jnp.dot(b_ref[...], w_ref[ka:, :], preferred_element_type=_F32)
    o_ref[...] = x_ref[...] + acc


def _out_proj(a, b, w, x):
    n, d = x.shape
    tm = _tile(n, 1024)
    tn = 512
    return pl.pallas_call(
        _out_proj_kernel,
        grid=(n // tm, d // tn),
        in_specs=[
            pl.BlockSpec((tm, a.shape[1]), lambda i, j: (i, 0)),
            pl.BlockSpec((tm, b.shape[1]), lambda i, j: (i, 0)),
            pl.BlockSpec((w.shape[0], tn), lambda i, j: (0, j)),
            pl.BlockSpec((tm, tn), lambda i, j: (i, j)),
        ],
        out_specs=pl.BlockSpec((tm, tn), lambda i, j: (i, j)),
        out_shape=jax.ShapeDtypeStruct((n, d), _F32),
        compiler_params=_cparams("parallel", "arbitrary"),
        name="out_proj",
    )(a, b, w, x)


def _swiglu_acc(h, wg, wu, wd):
    gate = jnp.dot(h, wg, preferred_element_type=_F32)
    up = jnp.dot(h, wu, preferred_element_type=_F32)
    act = (gate / (1.0 + jnp.exp(-gate)) * up).astype(_BF16)
    return jnp.dot(act, wd, preferred_element_type=_F32)


def _ffn_kernel(x_ref, g_ref, wg_ref, wu_ref, wd_ref, o_ref, h_sc):
    @pl.when(pl.program_id(1) == 0)
    def _():
        x = x_ref[...]
        h_sc[...] = _rms(x, g_ref[...]).astype(_BF16)
        o_ref[...] = x

    o_ref[...] += _swiglu_acc(h_sc[...], wg_ref[...], wu_ref[...], wd_ref[...])


def _ffn(x, g, wg, wu, wd):
    n, d = x.shape
    f = wg.shape[1]
    tm = _tile(n, 512)
    tf = 512
    return pl.pallas_call(
        _ffn_kernel,
        grid=(n // tm, f // tf),
        in_specs=[
            pl.BlockSpec((tm, d), lambda i, j: (i, 0)),
            pl.BlockSpec((1, d), lambda i, j: (0, 0)),
            pl.BlockSpec((d, tf), lambda i, j: (0, j)),
            pl.BlockSpec((d, tf), lambda i, j: (0, j)),
            pl.BlockSpec((tf, d), lambda i, j: (j, 0)),
        ],
        out_specs=pl.BlockSpec((tm, d), lambda i, j: (i, 0)),
        out_shape=jax.ShapeDtypeStruct((n, d), _F32),
        scratch_shapes=[pltpu.VMEM((tm, d), _BF16)],
        compiler_params=_cparams("parallel", "arbitrary"),
        name="ffn_swiglu",
    )(x, g, wg, wu, wd)


def _pool_kernel(pos0_ref, slen_ref, prev_ref, cur_ref, next_ref, w_ref, scale_ref, o_ref, buf_sc):
    i = pl.program_id(0)
    t = cur_ref.shape[0]
    pos0 = pos0_ref[i]
    slen = slen_ref[i]
    cur = cur_ref[...].astype(_F32)
    buf_sc[0:HALO, :] = jnp.where(pos0 > 0, prev_ref[...].astype(_F32), 0.0)
    buf_sc[HALO:HALO + t, :] = cur
    buf_sc[HALO + t:, :] = jnp.where(pos0 + t < slen, next_ref[...].astype(_F32), 0.0)
    tpos = pos0 + lax.broadcasted_iota(jnp.int32, (t, 1), 0)
    for g, win in enumerate(POOL_WINDOWS):
        c0, c1 = g * POOL_GROUP, (g + 1) * POOL_GROUP
        half = win // 2
        acc = buf_sc[HALO - half:HALO - half + t, c0:c1]
        for dlt in range(-half + 1, half):
            acc = acc + buf_sc[HALO + dlt:HALO + dlt + t, c0:c1]
        count = (jnp.minimum(tpos + half, slen) - jnp.maximum(tpos - half, 0)).astype(_F32)
        p = (acc / count - cur[:, c0:c1]).astype(_BF16)
        y = jnp.dot(p, w_ref[g], preferred_element_type=_F32)
        o_ref[:, c0:c1] = (y * scale_ref[:, c0:c1]).astype(o_ref.dtype)


def _pool_mixer(u, pool_w, pool_scale, tile_pos0, tile_slen, t):
    n = u.shape[0]
    hb = t // HALO
    last = n // HALO - 1
    grid_spec = pltpu.PrefetchScalarGridSpec(
        num_scalar_prefetch=2,
        grid=(n // t,),
        in_specs=[
            pl.BlockSpec((HALO, POOL_WIDTH), lambda i, p, s: (jnp.maximum(i * hb - 1, 0), 0)),
            pl.BlockSpec((t, POOL_WIDTH), lambda i, p, s: (i, 0)),
            pl.BlockSpec((HALO, POOL_WIDTH), lambda i, p, s: (jnp.minimum((i + 1) * hb, last), 0)),
            pl.BlockSpec((len(POOL_WINDOWS), POOL_GROUP, POOL_GROUP), lambda i, p, s: (0, 0, 0)),
            pl.BlockSpec((1, POOL_WIDTH), lambda i, p, s: (0, 0)),
        ],
        out_specs=pl.BlockSpec((t, POOL_WIDTH), lambda i, p, s: (i, 0)),
        scratch_shapes=[pltpu.VMEM((t + 2 * HALO, POOL_WIDTH), _F32)],
    )
    return pl.pallas_call(
        _pool_kernel,
        grid_spec=grid_spec,
        out_shape=jax.ShapeDtypeStruct((n, POOL_WIDTH), _BF16),
        compiler_params=_cparams("arbitrary"),
        name="pool_mixer",
    )(tile_pos0, tile_slen, u, u, u, pool_w, pool_scale)


def _mla_prep_kernel(ql_ref, kvl_ref, kr_ref, gq_ref, gkv_ref, wq_ref, wkv_ref, cos_ref, sin_ref,
                     q_ref, k_ref, v_ref):
    scale = (MLA_NOPE + MLA_ROPE) ** -0.5 * LOG2E
    cos = cos_ref[...]
    sin = sin_ref[...]
    lane = lax.broadcasted_iota(jnp.int32, cos.shape, 1)
    low = lane < MLA_ROPE

    qn = _rms(ql_ref[...].astype(_F32), gq_ref[...]).astype(_BF16)
    q = jnp.dot(qn, wq_ref[...], preferred_element_type=_F32)
    kvn = _rms(kvl_ref[...].astype(_F32), gkv_ref[...]).astype(_BF16)
    kv = jnp.dot(kvn, wkv_ref[...], preferred_element_type=_F32)
    nope_w = MLA_HEADS * MLA_NOPE
    v_ref[...] = kv[:, nope_w:].astype(v_ref.dtype)

    kr = kr_ref[...].astype(_F32)
    k_pe = jnp.where(low, kr * cos + _rope_partner(kr) * sin, 0.0).astype(k_ref.dtype)
    for pair in range(MLA_HEADS // 2):
        pe = q[:, nope_w + pair * LANES:nope_w + (pair + 1) * LANES]
        roped = (pe * cos + _rope_partner(pe) * sin) * scale
        for sub in range(2):
            h = 2 * pair + sub
            pe_h = roped if sub == 0 else pltpu.roll(roped, MLA_ROPE, 1)
            base = h * MLA_QK_PAD
            q_ref[:, base:base + MLA_NOPE] = (q[:, h * MLA_NOPE:(h + 1) * MLA_NOPE] * scale).astype(q_ref.dtype)
            q_ref[:, base + MLA_NOPE:base + MLA_QK_PAD] = jnp.where(low, pe_h, 0.0).astype(q_ref.dtype)
            k_ref[:, base:base + MLA_NOPE] = kv[:, h * MLA_NOPE:(h + 1) * MLA_NOPE].astype(k_ref.dtype)
            k_ref[:, base + MLA_NOPE:base + MLA_QK_PAD] = k_pe


def _mla_prep(u, gq, gkv, wq, wkv, cos_t, sin_t):
    n = u.shape[0]
    tm = _tile(n, 512)
    qk_w = MLA_HEADS * MLA_QK_PAD
    v_w = MLA_HEADS * MLA_V
    return pl.pallas_call(
        _mla_prep_kernel,
        grid=(n // tm,),
        in_specs=[
            pl.BlockSpec((tm, MLA_Q_LORA), lambda i: (i, POOL_WIDTH // MLA_Q_LORA)),
            pl.BlockSpec((tm, MLA_KV_LORA), lambda i: (i, (POOL_WIDTH + MLA_Q_LORA) // MLA_KV_LORA)),
            pl.BlockSpec((tm, LANES), lambda i: (i, (POOL_WIDTH + MLA_Q_LORA + MLA_KV_LORA) // LANES)),
            pl.BlockSpec((1, MLA_Q_LORA), lambda i: (0, 0)),
            pl.BlockSpec((1, MLA_KV_LORA), lambda i: (0, 0)),
            pl.BlockSpec(wq.shape, lambda i: (0, 0)),
            pl.BlockSpec(wkv.shape, lambda i: (0, 0)),
            pl.BlockSpec((tm, LANES), lambda i: (i, 0)),
            pl.BlockSpec((tm, LANES), lambda i: (i, 0)),
        ],
        out_specs=[
            pl.BlockSpec((tm, qk_w), lambda i: (i, 0)),
            pl.BlockSpec((tm, qk_w), lambda i: (i, 0)),
            pl.BlockSpec((tm, v_w), lambda i: (i, 0)),
        ],
        out_shape=[
            jax.ShapeDtypeStruct((n, qk_w), _BF16),
            jax.ShapeDtypeStruct((n, qk_w), _BF16),
            jax.ShapeDtypeStruct((n, v_w), _BF16),
        ],
        compiler_params=_cparams("parallel"),
        name="mla_prep",
    )(u, u, u, gq, gkv, wq, wkv, cos_t, sin_t)


def _gqa_prep_kernel(dq_ref, dk_ref, gq_ref, gk_ref, cos_ref, sin_ref, q_ref, k_ref):
    scale = GQA_HD ** -0.5 * LOG2E
    cos = cos_ref[...]
    sin = sin_ref[...]

    def norm_rope(x, g):
        xn = _rms(x.astype(_F32), g)
        return xn * cos + _rope_partner(xn) * sin

    for h in range(GQA_HEADS):
        sl = slice(h * GQA_HD, (h + 1) * GQA_HD)
        q_ref[:, sl] = (norm_rope(dq_ref[:, sl], gq_ref[...]) * scale).astype(q_ref.dtype)
    for h in range(GQA_KV):
        sl = slice(h * GQA_HD, (h + 1) * GQA_HD)
        k_ref[:, sl] = norm_rope(dk_ref[:, sl], gk_ref[...]).astype(k_ref.dtype)


def _gqa_prep(u, gq, gk, cos_t, sin_t):
    n = u.shape[0]
    tm = _tile(n, 512)
    q_w = GQA_HEADS * GQA_HD
    k_w = GQA_KV * GQA_HD
    dq_off = 3 * DIFF_HEADS * 2 * DIFF_HD
    return pl.pallas_call(
        _gqa_prep_kernel,
        grid=(n // tm,),
        in_specs=[
            pl.BlockSpec((tm, q_w), lambda i: (i, dq_off // q_w)),
            pl.BlockSpec((tm, k_w), lambda i: (i, (dq_off + q_w) // k_w)),
            pl.BlockSpec((1, GQA_HD), lambda i: (0, 0)),
            pl.BlockSpec((1, GQA_HD), lambda i: (0, 0)),
            pl.BlockSpec((tm, LANES), lambda i: (i, 0)),
            pl.BlockSpec((tm, LANES), lambda i: (i, 0)),
        ],
        out_specs=[
            pl.BlockSpec((tm, q_w), lambda i: (i, 0)),
            pl.BlockSpec((tm, k_w), lambda i: (i, 0)),
        ],
        out_shape=[
            jax.ShapeDtypeStruct((n, q_w), _BF16),
            jax.ShapeDtypeStruct((n, k_w), _BF16),
        ],
        compiler_params=_cparams("parallel"),
        name="gqa_prep",
    )(u, u, gq, gk, cos_t, sin_t)


def _flash_step(q_src, k_ref, v_ref, bias_refs, m_sc, l_sc, acc_sc, tq):
    tk = k_ref.shape[0]
    kh = min(tk, FLASH_KH)
    n_chunks = kh // LANES
    starts = range(0, m_sc.shape[0], FLASH_SUB)

    def score(half, r0):
        k = k_ref[half * kh:(half + 1) * kh, :]
        return lax.dot_general(q_src[r0:r0 + FLASH_SUB, :], k, (((1,), (1,)), ((), ())),
                               preferred_element_type=_F32)

    def softmax_pv(half, r0, s):
        sl = slice(r0, r0 + FLASH_SUB)
        chunks = [s[:, c * LANES:(c + 1) * LANES] for c in range(n_chunks)]
        if bias_refs is not None:
            b0 = r0 % tq
            chunks = [bias_refs[half][b0:b0 + FLASH_SUB, c * LANES:(c + 1) * LANES] + ch
                      for c, ch in enumerate(chunks)]
        m_prev = m_sc[sl, :]
        row_max = jnp.max(functools.reduce(jnp.maximum, chunks), axis=-1, keepdims=True)
        m_new = jnp.maximum(m_prev, row_max)
        alpha = jnp.exp2(m_prev - m_new)
        ps = [jnp.exp2(ch - m_new) for ch in chunks]
        l_sc[sl, :] = alpha * l_sc[sl, :] + functools.reduce(jnp.add, ps)
        p = jnp.concatenate(ps, axis=1).astype(_BF16)
        v = v_ref[half * kh:(half + 1) * kh, :]
        acc_sc[sl, :] = alpha * acc_sc[sl, :] + jnp.dot(p, v, preferred_element_type=_F32)
        m_sc[sl, :] = m_new

    scores = [score(0, r0) for r0 in starts]
    for half in range(tk // kh):
        nxt = []
        for r0, s in zip(starts, scores):
            softmax_pv(half, r0, s)
            if (half + 1) * kh < tk:
                nxt.append(score(half + 1, r0))
        scores = nxt


def _flash_kernel(*refs, diff, n_halves, lambda_init):
    if diff:
        q_ref, k_ref, v_ref = refs[:3]
        bias_refs = refs[3:3 + n_halves]
        (lq1_ref, lk1_ref, lq2_ref, lk2_ref, sub_ref, _, o_ref,
         q_sc, m_sc, l_sc, acc_sc) = refs[3 + n_halves:]
    else:
        q_ref, k_ref, v_ref, _, o_ref, m_sc, l_sc, acc_sc = refs
    ki = pl.program_id(3)
    tq = q_ref.shape[0]

    @pl.when(ki == 0)
    def _():
        m_sc[...] = jnp.full_like(m_sc, _NEG)
        l_sc[...] = jnp.zeros_like(l_sc)
        acc_sc[...] = jnp.zeros_like(acc_sc)
        if diff:
            q = q_ref[...]
            lane = lax.broadcasted_iota(jnp.int32, q.shape, 1)
            zero = jnp.zeros_like(q)
            q_sc[0:tq, :] = jnp.where(lane < DIFF_HD, q, zero)
            q_sc[tq:, :] = jnp.where(lane >= DIFF_HD, q, zero)

    if diff:
        _flash_step(q_sc, k_ref, v_ref, bias_refs, m_sc, l_sc, acc_sc, tq)
    else:
        _flash_step(q_ref, k_ref, v_ref, None, m_sc, l_sc, acc_sc, tq)

    @pl.when(ki == pl.num_programs(3) - 1)
    def _():
        o = acc_sc[...] / jnp.sum(l_sc[...], axis=-1, keepdims=True)
        if diff:
            lam = (jnp.exp(jnp.sum(lq1_ref[...] * lk1_ref[...], axis=-1, keepdims=True))
                   - jnp.exp(jnp.sum(lq2_ref[...] * lk2_ref[...], axis=-1, keepdims=True))
                   + lambda_init)
            o = o[0:tq, :] - lam * o[tq:, :]
            o = _rms(o, sub_ref[...]) * (1.0 - lambda_init)
        o_ref[...] = o.astype(o_ref.dtype)


def _flash(q_arr, k_arr, v_arr, prev_out, *, row0, n_seq, seq, heads, kv_group, dq, dv,
           q_col0, k_col0, v_col0, tq, tk, diff_args=None, lambda_init=0.0, name):
    n = q_arr.shape[0]
    nq, nk = seq // tq, seq // tk
    diff = diff_args is not None
    rows = 2 * tq if diff else tq
    assert rows % FLASH_SUB == 0 and tq % FLASH_SUB == 0

    def q_map(b, h, qi, ki):
        return (row0 // tq + b * nq + qi, q_col0 + h)

    def k_map(b, h, qi, ki):
        return (row0 // tk + b * nk + ki, k_col0 + h // kv_group)

    def v_map(b, h, qi, ki):
        return (row0 // tk + b * nk + ki, v_col0 + h // kv_group)

    def o_map(b, h, qi, ki):
        return (row0 // tq + b * nq + qi, h)

    in_specs = [pl.BlockSpec((tq, dq), q_map), pl.BlockSpec((tk, dq), k_map), pl.BlockSpec((tk, dv), v_map)]
    args = [q_arr, k_arr, v_arr]
    scratch = []
    kh = min(tk, FLASH_KH)
    n_halves = tk // kh
    if diff:
        assert kh % tq == 0 and tq >= REL_MAX_DIST
        ratio = kh // tq
        bias, lq1, lk1, lq2, lk2, sub = diff_args
        assert bias.shape[1:] == (ratio + 4, tq, kh)

        def bias_map(half):
            def index_map(b, h, qi, ki):
                off = ratio * (n_halves * ki + half) - qi
                return (h, jnp.clip(off, -ratio - 1, 2) + ratio + 1, 0, 0)
            return index_map

        for half in range(n_halves):
            in_specs.append(pl.BlockSpec((None, None, tq, kh), bias_map(half)))
            args.append(bias)
        in_specs += [pl.BlockSpec((1, DIFF_HD), lambda b, h, qi, ki: (0, 0))] * 4
        in_specs.append(pl.BlockSpec((1, 2 * DIFF_HD), lambda b, h, qi, ki: (0, 0)))
        args += [lq1, lk1, lq2, lk2, sub]
        scratch.append(pltpu.VMEM((rows, dq), _BF16))
    out_shape = jax.ShapeDtypeStruct((n, heads * dv), _BF16)
    if prev_out is None:
        prev_out = jnp.zeros(out_shape.shape, out_shape.dtype)
    aliases = {len(args): 0}
    in_specs.append(pl.BlockSpec(memory_space=pl.ANY))
    args.append(prev_out)
    assert dv == LANES
    scratch += [pltpu.VMEM((rows, LANES), _F32), pltpu.VMEM((rows, LANES), _F32), pltpu.VMEM((rows, dv), _F32)]
    return pl.pallas_call(
        functools.partial(_flash_kernel, diff=diff, n_halves=n_halves, lambda_init=lambda_init),
        grid=(n_seq, heads, nq, nk),
        in_specs=in_specs,
        out_specs=pl.BlockSpec((tq, dv), o_map),
        out_shape=out_shape,
        scratch_shapes=scratch,
        input_output_aliases=aliases,
        compiler_params=_cparams("parallel", "parallel", "parallel", "arbitrary"),
        name=name,
    )(*args)


def _router_kernel(x_ref, g_ref, w_ref, b_ref, h_ref, route_ref):
    h = _rms(x_ref[...], g_ref[...])
    h_ref[...] = h
    logits = jnp.dot(h, w_ref[...], preferred_element_type=_F32, precision=lax.Precision.HIGHEST) + b_ref[...]
    lane = lax.broadcasted_iota(jnp.int32, logits.shape, 1)
    logits = jnp.where(lane < N_EXPERTS, logits, _NEG)
    v1 = jnp.max(logits, axis=-1, keepdims=True)
    i1 = jnp.min(jnp.where(logits == v1, lane, LANES), axis=-1, keepdims=True)
    rest = jnp.where(lane == i1, _NEG, logits)
    v2 = jnp.max(rest, axis=-1, keepdims=True)
    i2 = jnp.min(jnp.where(rest == v2, lane, LANES), axis=-1, keepdims=True)
    g1 = 1.0 / (1.0 + jnp.exp(v2 - v1))
    g2 = 1.0 - g1
    route = jnp.where(lane == 0, i1.astype(_F32),
                      jnp.where(lane == 1, i2.astype(_F32),
                                jnp.where(lane == 2, g1, jnp.where(lane == 3, g2, 0.0))))
    route_ref[...] = route


def _router(x, g, w_pad, b_pad):
    n, d = x.shape
    tm = _tile(n, 512)
    return pl.pallas_call(
        _router_kernel,
        grid=(n // tm,),
        in_specs=[
            pl.BlockSpec((tm, d), lambda i: (i, 0)),
            pl.BlockSpec((1, d), lambda i: (0, 0)),
            pl.BlockSpec((d, LANES), lambda i: (0, 0)),
            pl.BlockSpec((1, LANES), lambda i: (0, 0)),
        ],
        out_specs=[pl.BlockSpec((tm, d), lambda i: (i, 0)), pl.BlockSpec((tm, LANES), lambda i: (i, 0))],
        out_shape=[jax.ShapeDtypeStruct((n, d), _F32), jax.ShapeDtypeStruct((n, LANES), _F32)],
        compiler_params=_cparams("parallel"),
        name="moe_router",
    )(x, g, w_pad, b_pad)


def _gather_rows(src_hbm, idx_ref, idx0, stride, dst, n_rows, sem, *, wait):
    def step(i, carry):
        row = idx_ref[0, 0, idx0 + stride * i]
        copy = pltpu.make_async_copy(src_hbm.at[pl.ds(row, 1), :], dst.at[pl.ds(i, 1), :], sem)
        if wait:
            copy.wait()
        else:
            copy.start()
        return carry

    lax.fori_loop(0, n_rows, step, 0, unroll=8)


def _moe_ffn_kernel(te_ref, nused_ref, tok_ref, next_tok_ref, h_ref, wg_ref, wu_ref, wd_ref, ys_ref,
                    xf_sc, xb_sc, sem):
    r = pl.program_id(0)
    j = pl.program_id(1)
    n_used = nused_ref[0]
    used = r < n_used
    tm = xf_sc.shape[0]

    @pl.when(j == 0)
    def _():
        ys_ref[...] = jnp.zeros_like(ys_ref)

        @pl.when(r == 0)
        def _():
            _gather_rows(h_ref, tok_ref, 0, 1, xf_sc, tm, sem, wait=False)

        @pl.when(used)
        def _():
            _gather_rows(h_ref, tok_ref, 0, 1, xf_sc, tm, sem, wait=True)
            xb_sc[...] = xf_sc[...].astype(_BF16)

    @pl.when(jnp.logical_and(j == 1, r + 1 < n_used))
    def _():
        _gather_rows(h_ref, next_tok_ref, 0, 1, xf_sc, tm, sem, wait=False)

    @pl.when(used)
    def _():
        ys_ref[...] += _swiglu_acc(xb_sc[...], wg_ref[...], wu_ref[...], wd_ref[...])


def _moe_ffn(h, row_token3, tile_expert, n_used, wg, wu, wd):
    n_tiles, _, tm = row_token3.shape
    d = h.shape[1]
    f = wg.shape[2]
    tf = 512
    assert f // tf >= 2
    grid_spec = pltpu.PrefetchScalarGridSpec(
        num_scalar_prefetch=2,
        grid=(n_tiles, f // tf),
        in_specs=[
            pl.BlockSpec((1, 1, tm), lambda r, j, te, nu: (r, 0, 0), memory_space=pltpu.SMEM),
            pl.BlockSpec((1, 1, tm), lambda r, j, te, nu: (jnp.minimum(r + 1, n_tiles - 1), 0, 0),
                         memory_space=pltpu.SMEM),
            pl.BlockSpec(memory_space=pl.ANY),
            pl.BlockSpec((None, d, tf), lambda r, j, te, nu: (te[r], 0, j)),
            pl.BlockSpec((None, d, tf), lambda r, j, te, nu: (te[r], 0, j)),
            pl.BlockSpec((None, tf, d), lambda r, j, te, nu: (te[r], j, 0)),
        ],
        out_specs=pl.BlockSpec((tm, d), lambda r, j, te, nu: (r, 0)),
        scratch_shapes=[pltpu.VMEM((tm, d), _F32), pltpu.VMEM((tm, d), _BF16), pltpu.SemaphoreType.DMA(())],
    )
    return pl.pallas_call(
        _moe_ffn_kernel,
        grid_spec=grid_spec,
        out_shape=jax.ShapeDtypeStruct((n_tiles * tm, d), _F32),
        compiler_params=_cparams("arbitrary", "arbitrary"),
        name="moe_expert_ffn",
    )(tile_expert, n_used, row_token3, row_token3, h, wg, wu, wd)


def _combine_kernel(pos_ref, route_ref, x_ref, g_ref, ys_ref, o_ref, buf, sem):
    tc = x_ref.shape[0]
    for wait in (False, True):
        for k in range(TOP_K):
            _gather_rows(ys_ref, pos_ref, k, TOP_K, buf.at[k], tc, sem.at[k], wait=wait)
    route = route_ref[...]
    y = x_ref[...] + route[:, 2:3] * buf[0] + route[:, 3:4] * buf[1]
    o_ref[...] = _rms(y, g_ref[...])


def _combine(pos3, route, x, g, ys):
    n, d = x.shape
    n_chunks, _, width = pos3.shape
    tc = width // TOP_K
    return pl.pallas_call(
        _combine_kernel,
        grid=(n_chunks,),
        in_specs=[
            pl.BlockSpec((1, 1, width), lambda c: (c, 0, 0), memory_space=pltpu.SMEM),
            pl.BlockSpec((tc, LANES), lambda c: (c, 0)),
            pl.BlockSpec((tc, d), lambda c: (c, 0)),
            pl.BlockSpec((1, d), lambda c: (0, 0)),
            pl.BlockSpec(memory_space=pl.ANY),
        ],
        out_specs=pl.BlockSpec((tc, d), lambda c: (c, 0)),
        out_shape=jax.ShapeDtypeStruct((n, d), _F32),
        scratch_shapes=[pltpu.VMEM((TOP_K, tc, d), _F32), pltpu.SemaphoreType.DMA((TOP_K,))],
        compiler_params=_cparams("arbitrary"),
        name="moe_combine",
    )(pos3, route, x, g, ys)


def _rope_tables(pos, dim):
    inv = ROPE_THETA ** (-jnp.arange(0, dim, 2, dtype=_F32) / dim)
    ang = pos.astype(_F32)[:, None] * inv[None, :]
    c, s = jnp.cos(ang), jnp.sin(ang)
    return jnp.concatenate([c, c], axis=-1), jnp.concatenate([-s, s], axis=-1)


def _t5_bucket(rel):
    nb = REL_BUCKETS // 2
    max_exact = nb // 2
    ret = jnp.where(rel > 0, nb, 0)
    n = jnp.abs(rel)
    n_f = jnp.maximum(n, 1).astype(_F32)
    large = max_exact + (jnp.log(n_f / max_exact) / math.log(REL_MAX_DIST / max_exact)
                         * (nb - max_exact)).astype(jnp.int32)
    large = jnp.minimum(large, nb - 1)
    return ret + jnp.where(n < max_exact, n, large)


def _bias_tiles(rel_bias, tq, tk):
    ratio = tk // tq
    table = rel_bias.astype(_F32) * LOG2E
    off = jnp.arange(-ratio, 2)[:, None, None]
    rel = off * tq + jnp.arange(tk)[None, None, :] - jnp.arange(tq)[None, :, None]
    onehot = (_t5_bucket(rel)[..., None] == jnp.arange(REL_BUCKETS)).astype(_F32)
    near = jnp.einsum("dqkb,bh->hdqk", onehot, table, precision=lax.Precision.HIGHEST)
    far = table[_t5_bucket(jnp.array([-REL_MAX_DIST, REL_MAX_DIST]))].T
    far = jnp.broadcast_to(far[:, :, None, None], (far.shape[0], 2, tq, tk))
    return jnp.concatenate([far[:, :1], near, far[:, 1:]], axis=1)


def _moe_plan(experts, tm, n_tiles):
    e_flat = experts.reshape(-1)
    onehot = (e_flat[:, None] == jnp.arange(N_EXPERTS)[None, :]).astype(jnp.int32)
    csum = jnp.cumsum(onehot, axis=0)
    counts = csum[-1]
    rank = jnp.sum(onehot * (csum - 1), axis=1)
    padded = ((counts + tm - 1) // tm) * tm
    ends = jnp.cumsum(padded)
    starts = ends - padded
    pos = starts[e_flat] + rank
    tile_start = jnp.arange(n_tiles, dtype=jnp.int32) * tm
    tile_expert = jnp.minimum(jnp.sum(tile_start[:, None] >= ends[None, :], axis=1), N_EXPERTS - 1)
    n_used = (ends[-1] // tm).reshape(1)
    pair_token = jnp.arange(e_flat.shape[0], dtype=jnp.int32) // TOP_K
    row_token = jnp.zeros((n_tiles * tm,), jnp.int32).at[pos].set(pair_token, unique_indices=True)
    return pos.astype(jnp.int32), row_token, tile_expert.astype(jnp.int32), n_used.astype(jnp.int32)


def kernel(x_prompt, x_sample, rel_bias, norm_mix0, w_in0, pool_w, pool_scale, q_a_norm, w_q_up, kv_a_norm, w_kv_up, w_out0, norm_ffn0, ffn_w_gate, ffn_w_up, ffn_w_down, norm_mix1, w_in1, lambda_q1, lambda_k1, lambda_q2, lambda_k2, diff_subln, gqa_q_norm, gqa_k_norm, w_out1, norm_ffn1, router_w, router_b, moe_w_gate, moe_w_up, moe_w_down, final_norm):
    d = D_MODEL
    bp, sp, _ = x_prompt.shape
    bs, ss, _ = x_sample.shape
    n_p, n_s = bp * sp, bs * ss
    n = n_p + n_s
    groups = [(0, bp, sp), (n_p, bs, ss)]
    x = jnp.concatenate([x_prompt.reshape(n_p, d), x_sample.reshape(n_s, d)], axis=0)
    pos = jnp.concatenate([jnp.tile(jnp.arange(sp), bp), jnp.tile(jnp.arange(ss), bs)])
    row = lambda v: v.reshape(1, -1).astype(_F32)

    w_in0_p = jnp.pad(w_in0[0], ((0, 0), (0, IN0_PAD - IN0_WIDTH))).astype(_BF16)
    u0 = _rms_matmul(x, row(norm_mix0[0]), w_in0_p, tn=640)

    t_pool = _tile(math.gcd(sp, ss), 512)
    tile_rows = jnp.arange(n // t_pool) * t_pool
    tile_pos0 = pos[tile_rows].astype(jnp.int32)
    tile_slen = jnp.where(tile_rows < n_p, sp, ss).astype(jnp.int32)
    a = _pool_mixer(u0, pool_w[0].astype(_BF16), row(pool_scale[0]), tile_pos0, tile_slen, t_pool)

    wq = w_q_up[0].reshape(MLA_Q_LORA, MLA_HEADS, MLA_NOPE + MLA_ROPE)
    wq = jnp.concatenate([wq[:, :, :MLA_NOPE].reshape(MLA_Q_LORA, -1),
                          wq[:, :, MLA_NOPE:].reshape(MLA_Q_LORA, -1)], axis=1).astype(_BF16)
    wkv = w_kv_up[0].reshape(MLA_KV_LORA, MLA_HEADS, MLA_NOPE + MLA_V)
    wkv = jnp.concatenate([wkv[:, :, :MLA_NOPE].reshape(MLA_KV_LORA, -1),
                           wkv[:, :, MLA_NOPE:].reshape(MLA_KV_LORA, -1)], axis=1).astype(_BF16)
    cos_m, sin_m = _rope_tables(pos, MLA_ROPE)
    cos_m = jnp.concatenate([cos_m, cos_m], axis=-1)
    sin_m = jnp.concatenate([sin_m, sin_m], axis=-1)
    q0, k0, v0 = _mla_prep(u0, row(q_a_norm[0]), row(kv_a_norm[0]), wq, wkv, cos_m, sin_m)
    b = None
    for gi, (row0, n_seq, seq) in enumerate(groups):
        b = _flash(q0, k0, v0, b, row0=row0, n_seq=n_seq, seq=seq, heads=MLA_HEADS, kv_group=1,
                   dq=MLA_QK_PAD, dv=MLA_V, q_col0=0, k_col0=0, v_col0=0, tq=_tile(seq, FLASH_TQ),
                   tk=_tile(seq, FLASH_TK), name=f"mla_attn_{gi}")
    x = _out_proj(a, b, w_out0[0].astype(_BF16), x)
    x = _ffn(x, row(norm_ffn0[0]), ffn_w_gate[0].astype(_BF16), ffn_w_up[0].astype(_BF16),
             ffn_w_down[0].astype(_BF16))

    lambda_init = 0.8 - 0.6 * math.exp(-0.3 * 1)
    cw = DIFF_HEADS * 2 * DIFF_HD
    col_scale = jnp.concatenate([jnp.full((cw,), DIFF_HD ** -0.5 * LOG2E, _F32),
                                 jnp.ones((w_in1.shape[2] - cw,), _F32)])
    u1 = _rms_matmul(x, row(norm_mix1[0]), (w_in1[0] * col_scale[None, :]).astype(_BF16), tn=512)

    c = None
    for gi, (row0, n_seq, seq) in enumerate(groups):
        tq, tk = _tile(seq, DIFF_TQ), _tile(seq, FLASH_TK)
        diff_args = (_bias_tiles(rel_bias, tq, min(tk, FLASH_KH)), row(lambda_q1[0]), row(lambda_k1[0]),
                     row(lambda_q2[0]), row(lambda_k2[0]), row(diff_subln[0]))
        c = _flash(u1, u1, u1, c, row0=row0, n_seq=n_seq, seq=seq, heads=DIFF_HEADS, kv_group=1,
                   dq=2 * DIFF_HD, dv=2 * DIFF_HD, q_col0=0, k_col0=DIFF_HEADS, v_col0=2 * DIFF_HEADS,
                   tq=tq, tk=tk, diff_args=diff_args, lambda_init=lambda_init, name=f"diff_attn_{gi}")

    cr, sr = _rope_tables(pos // GRID_W, AXIAL_DIM)
    cc, sc = _rope_tables(pos % GRID_W, AXIAL_DIM)
    cos_a = jnp.concatenate([cr, cc], axis=-1)
    sin_a = jnp.concatenate([sr, sc], axis=-1)
    qg, kg = _gqa_prep(u1, row(gqa_q_norm[0]), row(gqa_k_norm[0]), cos_a, sin_a)
    v_col0 = (3 * cw + (GQA_HEADS + GQA_KV) * GQA_HD) // GQA_HD
    dd = None
    for gi, (row0, n_seq, seq) in enumerate(groups):
        dd = _flash(qg, kg, u1, dd, row0=row0, n_seq=n_seq, seq=seq, heads=GQA_HEADS, kv_group=GQA_GROUP,
                    dq=GQA_HD, dv=GQA_HD, q_col0=0, k_col0=0, v_col0=v_col0, tq=_tile(seq, FLASH_TQ),
                    tk=_tile(seq, FLASH_TK), name=f"gqa_attn_{gi}")
    x = _out_proj(c, dd, w_out1[0].astype(_BF16), x)

    w_r = jnp.pad(router_w[0], ((0, 0), (0, LANES - N_EXPERTS)))
    b_r = jnp.pad(router_b[0], (0, LANES - N_EXPERTS)).reshape(1, LANES)
    h2, route = _router(x, row(norm_ffn1[0]), w_r, b_r)
    experts = route[:, :TOP_K].astype(jnp.int32)
    tm_moe = _tile(n * TOP_K, 1024)
    n_tiles = n * TOP_K // tm_moe + N_EXPERTS
    slot, row_token, tile_expert, n_used = _moe_plan(experts, tm_moe, n_tiles)
    tc = _tile(n, 256)
    pos3 = slot.reshape(n // tc, 1, tc * TOP_K)
    ys = _moe_ffn(h2, row_token.reshape(n_tiles, 1, tm_moe), tile_expert, n_used,
                  moe_w_gate[0].astype(_BF16), moe_w_up[0].astype(_BF16), moe_w_down[0].astype(_BF16))
    y = _combine(pos3, route, x, row(final_norm), ys)
    return (y[:n_p].reshape(bp, sp, d), y[n_p:].reshape(bs, ss, d))
```

```python
import functools
import math

import jax
import jax.numpy as jnp
from jax import lax
from jax.experimental import pallas as pl
from jax.experimental.pallas import tpu as pltpu

D_MODEL = 2048
RMS_EPS = 1e-6
ROPE_THETA = 10000.0
GRID_W = 64
POOL_WINDOWS = (2, 4, 8, 16)
POOL_GROUP = D_MODEL // 8
POOL_WIDTH = POOL_GROUP * len(POOL_WINDOWS)
MLA_HEADS = 8
MLA_Q_LORA = D_MODEL // 4
MLA_KV_LORA = D_MODEL // 8
MLA_NOPE = 128
MLA_ROPE = 64
MLA_V = 128
MLA_QK_PAD = 256
IN0_WIDTH = POOL_WIDTH + MLA_Q_LORA + MLA_KV_LORA + MLA_ROPE
IN0_PAD = 1920
DIFF_HEADS = 8
DIFF_HD = 64
GQA_HEADS = 8
GQA_KV = 2
GQA_GROUP = GQA_HEADS // GQA_KV
GQA_HD = 128
AXIAL_DIM = GQA_HD // 2
REL_BUCKETS = 32
REL_MAX_DIST = 128
N_EXPERTS = 8
TOP_K = 2
LANES = 128
HALO = 16
FLASH_SUB = 256
FLASH_TQ = 2048
FLASH_TK = 2048
FLASH_KH = 1024
DIFF_TQ = 512
LOG2E = math.log2(math.e)

_F32 = jnp.float32
_BF16 = jnp.bfloat16
_VMEM_LIMIT = 56 * 1024 * 1024
_NEG = -1e30


def _cparams(*sem):
    return pltpu.CompilerParams(dimension_semantics=sem, vmem_limit_bytes=_VMEM_LIMIT)


def _tile(n, pref):
    t = pref
    while n % t:
        t //= 2
    return t


def _rms(x, g):
    ms = jnp.mean(x * x, axis=-1, keepdims=True)
    return x * lax.rsqrt(ms + RMS_EPS) * g


def _rope_partner(x):
    lane = lax.broadcasted_iota(jnp.int32, x.shape, 1)
    first = (lane % 64) < 32
    return jnp.where(first, pltpu.roll(x, 96, 1), pltpu.roll(x, 32, 1))


def _rms_matmul_kernel(x_ref, g_ref, w_ref, o_ref, h_sc):
    @pl.when(pl.program_id(1) == 0)
    def _():
        h_sc[...] = _rms(x_ref[...], g_ref[...]).astype(_BF16)

    o_ref[...] = jnp.dot(h_sc[...], w_ref[...], preferred_element_type=_F32).astype(o_ref.dtype)


def _rms_matmul(x, g, w, tn):
    n, d = x.shape
    width = w.shape[1]
    tm = _tile(n, 1024)
    return pl.pallas_call(
        _rms_matmul_kernel,
        grid=(n // tm, width // tn),
        in_specs=[
            pl.BlockSpec((tm, d), lambda i, j: (i, 0)),
            pl.BlockSpec((1, d), lambda i, j: (0, 0)),
            pl.BlockSpec((d, tn), lambda i, j: (0, j)),
        ],
        out_specs=pl.BlockSpec((tm, tn), lambda i, j: (i, j)),
        out_shape=jax.ShapeDtypeStruct((n, width), _BF16),
        scratch_shapes=[pltpu.VMEM((tm, d), _BF16)],
        compiler_params=_cparams("parallel", "arbitrary"),
        name="rms_matmul",
    )(x, g, w)


def _out_proj_kernel(a_ref, b_ref, w_ref, x_ref, o_ref):
    ka = a_ref.shape[1]
    acc = jnp.dot(a_ref[...], w_ref[:ka, :], preferred_element_type=_F32)
    acc = acc + jnp.dot(b_ref[...], w_ref[ka:, :], preferred_element_type=_F32)
    o_ref[...] = x_ref[...] + acc


def _out_proj(a, b, w, x):
    n, d = x.shape
    tm = _tile(n, 512)
    tn = d
    return pl.pallas_call(
        _out_proj_kernel,
        grid=(n // tm, d // tn),
        in_specs=[
            pl.BlockSpec((tm, a.shape[1]), lambda i, j: (i, 0)),
            pl.BlockSpec((tm, b.shape[1]), lambda i, j: (i, 0)),
            pl.BlockSpec((w.shape[0], tn), lambda i, j: (0, j)),
            pl.BlockSpec((tm, tn), lambda i, j: (i, j)),
        ],
        out_specs=pl.BlockSpec((tm, tn), lambda i, j: (i, j)),
        out_shape=jax.ShapeDtypeStruct((n, d), _F32),
        compiler_params=_cparams("parallel", "arbitrary"),
        name="out_proj",
    )(a, b, w, x)


def _swiglu_acc(h, wg, wu, wd):
    gate = jnp.dot(h, wg, preferred_element_type=_F32)
    up = jnp.dot(h, wu, preferred_element_type=_F32)
    act = (gate / (1.0 + jnp.exp(-gate)) * up).astype(_BF16)
    return jnp.dot(act, wd, preferred_element_type=_F32)


def _ffn_kernel(x_ref, g_ref, wg_ref, wu_ref, wd_ref, o_ref, h_sc):
    @pl.when(pl.program_id(1) == 0)
    def _():
        x = x_ref[...]
        h_sc[...] = _rms(x, g_ref[...]).astype(_BF16)
        o_ref[...] = x

    o_ref[...] += _swiglu_acc(h_sc[...], wg_ref[...], wu_ref[...], wd_ref[...])


def _ffn(x, g, wg, wu, wd):
    n, d = x.shape
    f = wg.shape[1]
    tm = _tile(n, 512)
    tf = 512
    return pl.pallas_call(
        _ffn_kernel,
        grid=(n // tm, f // tf),
        in_specs=[
            pl.BlockSpec((tm, d), lambda i, j: (i, 0)),
            pl.BlockSpec((1, d), lambda i, j: (0, 0)),
            pl.BlockSpec((d, tf), lambda i, j: (0, j)),
            pl.BlockSpec((d, tf), lambda i, j: (0, j)),
            pl.BlockSpec((tf, d), lambda i, j: (j, 0)),
        ],
        out_specs=pl.BlockSpec((tm, d), lambda i, j: (i, 0)),
        out_shape=jax.ShapeDtypeStruct((n, d), _F32),
        scratch_shapes=[pltpu.VMEM((tm, d), _BF16)],
        compiler_params=_cparams("parallel", "arbitrary"),
        name="ffn_swiglu",
    )(x, g, wg, wu, wd)


def _pool_kernel(pos0_ref, slen_ref, prev_ref, cur_ref, next_ref, w_ref, scale_ref, o_ref, buf_sc):
    i = pl.program_id(0)
    t = cur_ref.shape[0]
    pos0 = pos0_ref[i]
    slen = slen_ref[i]
    cur = cur_ref[...].astype(_F32)
    buf_sc[0:HALO, :] = jnp.where(pos0 > 0, prev_ref[...].astype(_F32), 0.0)
    buf_sc[HALO:HALO + t, :] = cur
    buf_sc[HALO + t:, :] = jnp.where(pos0 + t < slen, next_ref[...].astype(_F32), 0.0)
    tpos = pos0 + lax.broadcasted_iota(jnp.int32, (t, 1), 0)
    for g, win in enumerate(POOL_WINDOWS):
        c0, c1 = g * POOL_GROUP, (g + 1) * POOL_GROUP
        half = win // 2
        acc = buf_sc[HALO - half:HALO - half + t, c0:c1]
        for dlt in range(-half + 1, half):
            acc = acc + buf_sc[HALO + dlt:HALO + dlt + t, c0:c1]
        count = (jnp.minimum(tpos + half, slen) - jnp.maximum(tpos - half, 0)).astype(_F32)
        p = (acc / count - cur[:, c0:c1]).astype(_BF16)
        y = jnp.dot(p, w_ref[g], preferred_element_type=_F32)
        o_ref[:, c0:c1] = (y * scale_ref[:, c0:c1]).astype(o_ref.dtype)


def _pool_mixer(u, pool_w, pool_scale, tile_pos0, tile_slen, t):
    n = u.shape[0]
    hb = t // HALO
    last = n // HALO - 1
    grid_spec = pltpu.PrefetchScalarGridSpec(
        num_scalar_prefetch=2,
        grid=(n // t,),
        in_specs=[
            pl.BlockSpec((HALO, POOL_WIDTH), lambda i, p, s: (jnp.maximum(i * hb - 1, 0), 0)),
            pl.BlockSpec((t, POOL_WIDTH), lambda i, p, s: (i, 0)),
            pl.BlockSpec((HALO, POOL_WIDTH), lambda i, p, s: (jnp.minimum((i + 1) * hb, last), 0)),
            pl.BlockSpec((len(POOL_WINDOWS), POOL_GROUP, POOL_GROUP), lambda i, p, s: (0, 0, 0)),
            pl.BlockSpec((1, POOL_WIDTH), lambda i, p, s: (0, 0)),
        ],
        out_specs=pl.BlockSpec((t, POOL_WIDTH), lambda i, p, s: (i, 0)),
        scratch_shapes=[pltpu.VMEM((t + 2 * HALO, POOL_WIDTH), _F32)],
    )
    return pl.pallas_call(
        _pool_kernel,
        grid_spec=grid_spec,
        out_shape=jax.ShapeDtypeStruct((n, POOL_WIDTH), _BF16),
        compiler_params=_cparams("arbitrary"),
        name="pool_mixer",
    )(tile_pos0, tile_slen, u, u, u, pool_w, pool_scale)


def _mla_prep_kernel(ql_ref, kvl_ref, kr_ref, gq_ref, gkv_ref, wq_ref, wkv_ref, cos_ref, sin_ref,
                     q_ref, k_ref, v_ref):
    scale = (MLA_NOPE + MLA_ROPE) ** -0.5 * LOG2E
    cos = cos_ref[...]
    sin = sin_ref[...]
    lane = lax.broadcasted_iota(jnp.int32, cos.shape, 1)
    low = lane < MLA_ROPE

    qn = _rms(ql_ref[...].astype(_F32), gq_ref[...]).astype(_BF16)
    q = jnp.dot(qn, wq_ref[...], preferred_element_type=_F32)
    kvn = _rms(kvl_ref[...].astype(_F32), gkv_ref[...]).astype(_BF16)
    kv = jnp.dot(kvn, wkv_ref[...], preferred_element_type=_F32)
    nope_w = MLA_HEADS * MLA_NOPE
    v_ref[...] = kv[:, nope_w:].astype(v_ref.dtype)

    kr = kr_ref[...].astype(_F32)
    k_pe = jnp.where(low, kr * cos + _rope_partner(kr) * sin, 0.0).astype(k_ref.dtype)
    for pair in range(MLA_HEADS // 2):
        pe = q[:, nope_w + pair * LANES:nope_w + (pair + 1) * LANES]
        roped = (pe * cos + _rope_partner(pe) * sin) * scale
        for sub in range(2):
            h = 2 * pair + sub
            pe_h = roped if sub == 0 else pltpu.roll(roped, MLA_ROPE, 1)
            base = h * MLA_QK_PAD
            q_ref[:, base:base + MLA_NOPE] = (q[:, h * MLA_NOPE:(h + 1) * MLA_NOPE] * scale).astype(q_ref.dtype)
            q_ref[:, base + MLA_NOPE:base + MLA_QK_PAD] = jnp.where(low, pe_h, 0.0).astype(q_ref.dtype)
            k_ref[:, base:base + MLA_NOPE] = kv[:, h * MLA_NOPE:(h + 1) * MLA_NOPE].astype(k_ref.dtype)
            k_ref[:, base + MLA_NOPE:base + MLA_QK_PAD] = k_pe


def _mla_prep(u, gq, gkv, wq, wkv, cos_t, sin_t):
    n = u.shape[0]
    tm = _tile(n, 512)
    qk_w = MLA_HEADS * MLA_QK_PAD
    v_w = MLA_HEADS * MLA_V
    return pl.pallas_call(
        _mla_prep_kernel,
        grid=(n // tm,),
        in_specs=[
            pl.BlockSpec((tm, MLA_Q_LORA), lambda i: (i, POOL_WIDTH // MLA_Q_LORA)),
            pl.BlockSpec((tm, MLA_KV_LORA), lambda i: (i, (POOL_WIDTH + MLA_Q_LORA) // MLA_KV_LORA)),
            pl.BlockSpec((tm, LANES), lambda i: (i, (POOL_WIDTH + MLA_Q_LORA + MLA_KV_LORA) // LANES)),
            pl.BlockSpec((1, MLA_Q_LORA), lambda i: (0, 0)),
            pl.BlockSpec((1, MLA_KV_LORA), lambda i: (0, 0)),
            pl.BlockSpec(wq.shape, lambda i: (0, 0)),
            pl.BlockSpec(wkv.shape, lambda i: (0, 0)),
            pl.BlockSpec((tm, LANES), lambda i: (i, 0)),
            pl.BlockSpec((tm, LANES), lambda i: (i, 0)),
        ],
        out_specs=[
            pl.BlockSpec((tm, qk_w), lambda i: (i, 0)),
            pl.BlockSpec((tm, qk_w), lambda i: (i, 0)),
            pl.BlockSpec((tm, v_w), lambda i: (i, 0)),
        ],
        out_shape=[
            jax.ShapeDtypeStruct((n, qk_w), _BF16),
            jax.ShapeDtypeStruct((n, qk_w), _BF16),
            jax.ShapeDtypeStruct((n, v_w), _BF16),
        ],
        compiler_params=_cparams("parallel"),
        name="mla_prep",
    )(u, u, u, gq, gkv, wq, wkv, cos_t, sin_t)


def _gqa_prep_kernel(dq_ref, dk_ref, gq_ref, gk_ref, cos_ref, sin_ref, q_ref, k_ref):
    scale = GQA_HD ** -0.5 * LOG2E
    cos = cos_ref[...]
    sin = sin_ref[...]

    def norm_rope(x, g):
        xn = _rms(x.astype(_F32), g)
        return xn * cos + _rope_partner(xn) * sin

    for h in range(GQA_HEADS):
        sl = slice(h * GQA_HD, (h + 1) * GQA_HD)
        q_ref[:, sl] = (norm_rope(dq_ref[:, sl], gq_ref[...]) * scale).astype(q_ref.dtype)
    for h in range(GQA_KV):
        sl = slice(h * GQA_HD, (h + 1) * GQA_HD)
        k_ref[:, sl] = norm_rope(dk_ref[:, sl], gk_ref[...]).astype(k_ref.dtype)


def _gqa_prep(u, gq, gk, cos_t, sin_t):
    n = u.shape[0]
    tm = _tile(n, 512)
    q_w = GQA_HEADS * GQA_HD
    k_w = GQA_KV * GQA_HD
    dq_off = 3 * DIFF_HEADS * 2 * DIFF_HD
    return pl.pallas_call(
        _gqa_prep_kernel,
        grid=(n // tm,),
        in_specs=[
            pl.BlockSpec((tm, q_w), lambda i: (i, dq_off // q_w)),
            pl.BlockSpec((tm, k_w), lambda i: (i, (dq_off + q_w) // k_w)),
            pl.BlockSpec((1, GQA_HD), lambda i: (0, 0)),
            pl.BlockSpec((1, GQA_HD), lambda i: (0, 0)),
            pl.BlockSpec((tm, LANES), lambda i: (i, 0)),
            pl.BlockSpec((tm, LANES), lambda i: (i, 0)),
        ],
        out_specs=[
            pl.BlockSpec((tm, q_w), lambda i: (i, 0)),
            pl.BlockSpec((tm, k_w), lambda i: (i, 0)),
        ],
        out_shape=[
            jax.ShapeDtypeStruct((n, q_w), _BF16),
            jax.ShapeDtypeStruct((n, k_w), _BF16),
        ],
        compiler_params=_cparams("parallel"),
        name="gqa_prep",
    )(u, u, gq, gk, cos_t, sin_t)


def _flash_step(q_src, k_ref, v_ref, bias_refs, m_sc, l_sc, acc_sc, tq):
    tk = k_ref.shape[0]
    kh = min(tk, FLASH_KH)
    n_chunks = kh // LANES
    starts = range(0, m_sc.shape[0], FLASH_SUB)

    def score(half, r0):
        k = k_ref[half * kh:(half + 1) * kh, :]
        return lax.dot_general(q_src[r0:r0 + FLASH_SUB, :], k, (((1,), (1,)), ((), ())),
                               preferred_element_type=_F32)

    def softmax_pv(half, r0, s):
        sl = slice(r0, r0 + FLASH_SUB)
        chunks = [s[:, c * LANES:(c + 1) * LANES] for c in range(n_chunks)]
        if bias_refs is not None:
            b0 = r0 % tq
            chunks = [bias_refs[half][b0:b0 + FLASH_SUB, c * LANES:(c + 1) * LANES] + ch
                      for c, ch in enumerate(chunks)]
        m_prev = m_sc[sl, :]
        row_max = jnp.max(functools.reduce(jnp.maximum, chunks), axis=-1, keepdims=True)
        m_new = jnp.maximum(m_prev, row_max)
        alpha = jnp.exp2(m_prev - m_new)
        ps = [jnp.exp2(ch - m_new) for ch in chunks]
        l_sc[sl, :] = alpha * l_sc[sl, :] + functools.reduce(jnp.add, ps)
        p = jnp.concatenate(ps, axis=1).astype(_BF16)
        v = v_ref[half * kh:(half + 1) * kh, :]
        acc_sc[sl, :] = alpha * acc_sc[sl, :] + jnp.dot(p, v, preferred_element_type=_F32)
        m_sc[sl, :] = m_new

    scores = [score(0, r0) for r0 in starts]
    for half in range(tk // kh):
        nxt = []
        for r0, s in zip(starts, scores):
            softmax_pv(half, r0, s)
            if (half + 1) * kh < tk:
                nxt.append(score(half + 1, r0))
        scores = nxt


def _flash_kernel(*refs, diff, n_halves, lambda_init):
    if diff:
        q_ref, k_ref, v_ref = refs[:3]
        bias_refs = refs[3:3 + n_halves]
        (lq1_ref, lk1_ref, lq2_ref, lk2_ref, sub_ref, _, o_ref,
         q_sc, m_sc, l_sc, acc_sc) = refs[3 + n_halves:]
    else:
        q_ref, k_ref, v_ref, _, o_ref, m_sc, l_sc, acc_sc = refs
    ki = pl.program_id(3)
    tq = q_ref.shape[0]

    @pl.when(ki == 0)
    def _():
        m_sc[...] = jnp.full_like(m_sc, _NEG)
        l_sc[...] = jnp.zeros_like(l_sc)
        acc_sc[...] = jnp.zeros_like(acc_sc)
        if diff:
            q = q_ref[...]
            lane = lax.broadcasted_iota(jnp.int32, q.shape, 1)
            zero = jnp.zeros_like(q)
            q_sc[0:tq, :] = jnp.where(lane < DIFF_HD, q, zero)
            q_sc[tq:, :] = jnp.where(lane >= DIFF_HD, q, zero)

    if diff:
        _flash_step(q_sc, k_ref, v_ref, bias_refs, m_sc, l_sc, acc_sc, tq)
    else:
        _flash_step(q_ref, k_ref, v_ref, None, m_sc, l_sc, acc_sc, tq)

    @pl.when(ki == pl.num_programs(3) - 1)
    def _():
        o = acc_sc[...] / jnp.sum(l_sc[...], axis=-1, keepdims=True)
        if diff:
            lam = (jnp.exp(jnp.sum(lq1_ref[...] * lk1_ref[...], axis=-1, keepdims=True))
                   - jnp.exp(jnp.sum(lq2_ref[...] * lk2_ref[...], axis=-1, keepdims=True))
                   + lambda_init)
            o = o[0:tq, :] - lam * o[tq:, :]
            o = _rms(o, sub_ref[...]) * (1.0 - lambda_init)
        o_ref[...] = o.astype(o_ref.dtype)


def _flash(q_arr, k_arr, v_arr, prev_out, *, row0, n_seq, seq, heads, kv_group, dq, dv,
           q_col0, k_col0, v_col0, tq, tk, diff_args=None, lambda_init=0.0, name):
    n = q_arr.shape[0]
    nq, nk = seq // tq, seq // tk
    diff = diff_args is not None
    rows = 2 * tq if diff else tq
    assert rows % FLASH_SUB == 0 and tq % FLASH_SUB == 0

    def q_map(b, h, qi, ki):
        return (row0 // tq + b * nq + qi, q_col0 + h)

    def k_map(b, h, qi, ki):
        return (row0 // tk + b * nk + ki, k_col0 + h // kv_group)

    def v_map(b, h, qi, ki):
        return (row0 // tk + b * nk + ki, v_col0 + h // kv_group)

    def o_map(b, h, qi, ki):
        return (row0 // tq + b * nq + qi, h)

    in_specs = [pl.BlockSpec((tq, dq), q_map), pl.BlockSpec((tk, dq), k_map), pl.BlockSpec((tk, dv), v_map)]
    args = [q_arr, k_arr, v_arr]
    scratch = []
    kh = min(tk, FLASH_KH)
    n_halves = tk // kh
    if diff:
        assert kh % tq == 0 and tq >= REL_MAX_DIST
        ratio = kh // tq
        bias, lq1, lk1, lq2, lk2, sub = diff_args
        assert bias.shape[1:] == (ratio + 4, tq, kh)

        def bias_map(half):
            def index_map(b, h, qi, ki):
                off = ratio * (n_halves * ki + half) - qi
                return (h, jnp.clip(off, -ratio - 1, 2) + ratio + 1, 0, 0)
            return index_map

        for half in range(n_halves):
            in_specs.append(pl.BlockSpec((None, None, tq, kh), bias_map(half)))
            args.append(bias)
        in_specs += [pl.BlockSpec((1, DIFF_HD), lambda b, h, qi, ki: (0, 0))] * 4
        in_specs.append(pl.BlockSpec((1, 2 * DIFF_HD), lambda b, h, qi, ki: (0, 0)))
        args += [lq1, lk1, lq2, lk2, sub]
        scratch.append(pltpu.VMEM((rows, dq), _BF16))
    out_shape = jax.ShapeDtypeStruct((n, heads * dv), _BF16)
    if prev_out is None:
        prev_out = jnp.zeros(out_shape.shape, out_shape.dtype)
    aliases = {len(args): 0}
    in_specs.append(pl.BlockSpec(memory_space=pl.ANY))
    args.append(prev_out)
    assert dv == LANES
    scratch += [pltpu.VMEM((rows, LANES), _F32), pltpu.VMEM((rows, LANES), _F32), pltpu.VMEM((rows, dv), _F32)]
    return pl.pallas_call(
        functools.partial(_flash_kernel, diff=diff, n_halves=n_halves, lambda_init=lambda_init),
        grid=(n_seq, heads, nq, nk),
        in_specs=in_specs,
        out_specs=pl.BlockSpec((tq, dv), o_map),
        out_shape=out_shape,
        scratch_shapes=scratch,
        input_output_aliases=aliases,
        compiler_params=_cparams("parallel", "parallel", "parallel", "arbitrary"),
        name=name,
    )(*args)


def _router_kernel(x_ref, g_ref, w_ref, b_ref, h_ref, route_ref):
    h = _rms(x_ref[...], g_ref[...])
    h_ref[...] = h
    logits = jnp.dot(h, w_ref[...], preferred_element_type=_F32, precision=lax.Precision.HIGHEST) + b_ref[...]
    lane = lax.broadcasted_iota(jnp.int32, logits.shape, 1)
    logits = jnp.where(lane < N_EXPERTS, logits, _NEG)
    v1 = jnp.max(logits, axis=-1, keepdims=True)
    i1 = jnp.min(jnp.where(logits == v1, lane, LANES), axis=-1, keepdims=True)
    rest = jnp.where(lane == i1, _NEG, logits)
    v2 = jnp.max(rest, axis=-1, keepdims=True)
    i2 = jnp.min(jnp.where(rest == v2, lane, LANES), axis=-1, keepdims=True)
    g1 = 1.0 / (1.0 + jnp.exp(v2 - v1))
    g2 = 1.0 - g1
    route = jnp.where(lane == 0, i1.astype(_F32),
                      jnp.where(lane == 1, i2.astype(_F32),
                                jnp.where(lane == 2, g1, jnp.where(lane == 3, g2, 0.0))))
    route_ref[...] = route


def _router(x, g, w_pad, b_pad):
    n, d = x.shape
    tm = _tile(n, 512)
    return pl.pallas_call(
        _router_kernel,
        grid=(n // tm,),
        in_specs=[
            pl.BlockSpec((tm, d), lambda i: (i, 0)),
            pl.BlockSpec((1, d), lambda i: (0, 0)),
            pl.BlockSpec((d, LANES), lambda i: (0, 0)),
            pl.BlockSpec((1, LANES), lambda i: (0, 0)),
        ],
        out_specs=[pl.BlockSpec((tm, d), lambda i: (i, 0)), pl.BlockSpec((tm, LANES), lambda i: (i, 0))],
        out_shape=[jax.ShapeDtypeStruct((n, d), _F32), jax.ShapeDtypeStruct((n, LANES), _F32)],
        compiler_params=_cparams("parallel"),
        name="moe_router",
    )(x, g, w_pad, b_pad)


def _gather_rows(src_hbm, idx_ref, idx0, stride, dst, n_rows, last, sem, *, wait):
    def step(i, carry):
        row = idx_ref[0, 0, idx0 + stride * jnp.minimum(i, last)]
        copy = pltpu.make_async_copy(src_hbm.at[pl.ds(row, 1), :], dst.at[pl.ds(i, 1), :], sem)
        if wait:
            copy.wait()
        else:
            copy.start()
        return carry

    lax.fori_loop(0, n_rows, step, 0, unroll=8)


def _moe_ffn_kernel(te_ref, nused_ref, tok_ref, next_tok_ref, h_ref, wg_ref, wu_ref, wd_ref, ys_ref,
                    xf_sc, xb_sc, sem):
    r = pl.program_id(0)
    j = pl.program_id(1)
    n_used = nused_ref[0]
    tm = xb_sc.shape[0]
    n_stage = xf_sc.shape[0]
    per_step = n_stage // pl.num_programs(1)

    @pl.when(j == 0)
    def _():
        ys_ref[...] = jnp.zeros_like(ys_ref)

        @pl.when(r == 0)
        def _():
            _gather_rows(h_ref, tok_ref, 0, 1, xf_sc, n_stage, tm - 1, sem, wait=False)

        @pl.when(r <= n_used)
        def _():
            _gather_rows(h_ref, tok_ref, 0, 1, xf_sc, n_stage, tm - 1, sem, wait=True)
            xb_sc[...] = xf_sc[0:tm, :].astype(_BF16)

    @pl.when(r < n_used)
    def _():
        for t in range(per_step):
            i = j * per_step + t
            row = next_tok_ref[0, 0, jnp.minimum(i, tm - 1)]
            pltpu.make_async_copy(h_ref.at[pl.ds(row, 1), :], xf_sc.at[pl.ds(i, 1), :], sem).start()
        ys_ref[...] += _swiglu_acc(xb_sc[...], wg_ref[...], wu_ref[...], wd_ref[...])


def _moe_ffn(h, row_token3, tile_expert, n_used, wg, wu, wd):
    n_tiles, _, tm = row_token3.shape
    d = h.shape[1]
    f = wg.shape[2]
    tf = 512
    n_steps = f // tf
    n_stage = -(-tm // n_steps) * n_steps
    grid_spec = pltpu.PrefetchScalarGridSpec(
        num_scalar_prefetch=2,
        grid=(n_tiles, n_steps),
        in_specs=[
            pl.BlockSpec((1, 1, tm), lambda r, j, te, nu: (r, 0, 0), memory_space=pltpu.SMEM),
            pl.BlockSpec((1, 1, tm), lambda r, j, te, nu: (jnp.minimum(r + 1, n_tiles - 1), 0, 0),
                         memory_space=pltpu.SMEM),
            pl.BlockSpec(memory_space=pl.ANY),
            pl.BlockSpec((None, d, tf), lambda r, j, te, nu: (te[r], 0, j)),
            pl.BlockSpec((None, d, tf), lambda r, j, te, nu: (te[r], 0, j)),
            pl.BlockSpec((None, tf, d), lambda r, j, te, nu: (te[r], j, 0)),
        ],
        out_specs=pl.BlockSpec((tm, d), lambda r, j, te, nu: (r, 0)),
        scratch_shapes=[pltpu.VMEM((n_stage, d), _F32), pltpu.VMEM((tm, d), _BF16),
                        pltpu.SemaphoreType.DMA(())],
    )
    return pl.pallas_call(
        _moe_ffn_kernel,
        grid_spec=grid_spec,
        out_shape=jax.ShapeDtypeStruct((n_tiles * tm, d), _F32),
        compiler_params=_cparams("arbitrary", "arbitrary"),
        name="moe_expert_ffn",
    )(tile_expert, n_used, row_token3, row_token3, h, wg, wu, wd)


def _combine_kernel(pos_ref, route_ref, x_ref, g_ref, ys_ref, o_ref, buf, sem):
    tc = x_ref.shape[0]
    for wait in (False, True):
        for k in range(TOP_K):
            _gather_rows(ys_ref, pos_ref, k, TOP_K, buf.at[k], tc, tc - 1, sem.at[k], wait=wait)
    route = route_ref[...]
    y = x_ref[...] + route[:, 2:3] * buf[0] + route[:, 3:4] * buf[1]
    o_ref[...] = _rms(y, g_ref[...])


def _combine(pos3, route, x, g, ys):
    n, d = x.shape
    n_chunks, _, width = pos3.shape
    tc = width // TOP_K
    return pl.pallas_call(
        _combine_kernel,
        grid=(n_chunks,),
        in_specs=[
            pl.BlockSpec((1, 1, width), lambda c: (c, 0, 0), memory_space=pltpu.SMEM),
            pl.BlockSpec((tc, LANES), lambda c: (c, 0)),
            pl.BlockSpec((tc, d), lambda c: (c, 0)),
            pl.BlockSpec((1, d), lambda c: (0, 0)),
            pl.BlockSpec(memory_space=pl.ANY),
        ],
        out_specs=pl.BlockSpec((tc, d), lambda c: (c, 0)),
        out_shape=jax.ShapeDtypeStruct((n, d), _F32),
        scratch_shapes=[pltpu.VMEM((TOP_K, tc, d), _F32), pltpu.SemaphoreType.DMA((TOP_K,))],
        compiler_params=_cparams("arbitrary"),
        name="moe_combine",
    )(pos3, route, x, g, ys)


def _rope_tables(pos, dim):
    inv = ROPE_THETA ** (-jnp.arange(0, dim, 2, dtype=_F32) / dim)
    ang = pos.astype(_F32)[:, None] * inv[None, :]
    c, s = jnp.cos(ang), jnp.sin(ang)
    return jnp.concatenate([c, c], axis=-1), jnp.concatenate([-s, s], axis=-1)


def _t5_bucket(rel):
    nb = REL_BUCKETS // 2
    max_exact = nb // 2
    ret = jnp.where(rel > 0, nb, 0)
    n = jnp.abs(rel)
    n_f = jnp.maximum(n, 1).astype(_F32)
    large = max_exact + (jnp.log(n_f / max_exact) / math.log(REL_MAX_DIST / max_exact)
                         * (nb - max_exact)).astype(jnp.int32)
    large = jnp.minimum(large, nb - 1)
    return ret + jnp.where(n < max_exact, n, large)


def _bias_tiles(rel_bias, tq, tk):
    ratio = tk // tq
    table = rel_bias.astype(_F32) * LOG2E
    off = jnp.arange(-ratio, 2)[:, None, None]
    rel = off * tq + jnp.arange(tk)[None, None, :] - jnp.arange(tq)[None, :, None]
    onehot = (_t5_bucket(rel)[..., None] == jnp.arange(REL_BUCKETS)).astype(_F32)
    near = jnp.einsum("dqkb,bh->hdqk", onehot, table, precision=lax.Precision.HIGHEST)
    far = table[_t5_bucket(jnp.array([-REL_MAX_DIST, REL_MAX_DIST]))].T
    far = jnp.broadcast_to(far[:, :, None, None], (far.shape[0], 2, tq, tk))
    return jnp.concatenate([far[:, :1], near, far[:, 1:]], axis=1)


def _moe_plan(experts, tm, n_tiles):
    e_flat = experts.reshape(-1)
    onehot = (e_flat[:, None] == jnp.arange(N_EXPERTS)[None, :]).astype(jnp.int32)
    csum = jnp.cumsum(onehot, axis=0)
    counts = csum[-1]
    rank = jnp.sum(onehot * (csum - 1), axis=1)
    padded = ((counts + tm - 1) // tm) * tm
    ends = jnp.cumsum(padded)
    starts = ends - padded
    pos = starts[e_flat] + rank
    tile_start = jnp.arange(n_tiles, dtype=jnp.int32) * tm
    tile_expert = jnp.minimum(jnp.sum(tile_start[:, None] >= ends[None, :], axis=1), N_EXPERTS - 1)
    n_used = (ends[-1] // tm).reshape(1)
    pair_token = jnp.arange(e_flat.shape[0], dtype=jnp.int32) // TOP_K
    row_token = jnp.zeros((n_tiles * tm,), jnp.int32).at[pos].set(pair_token, unique_indices=True)
    return pos.astype(jnp.int32), row_token, tile_expert.astype(jnp.int32), n_used.astype(jnp.int32)


def kernel(x_prompt, x_sample, rel_bias, norm_mix0, w_in0, pool_w, pool_scale, q_a_norm, w_q_up, kv_a_norm, w_kv_up, w_out0, norm_ffn0, ffn_w_gate, ffn_w_up, ffn_w_down, norm_mix1, w_in1, lambda_q1, lambda_k1, lambda_q2, lambda_k2, diff_subln, gqa_q_norm, gqa_k_norm, w_out1, norm_ffn1, router_w, router_b, moe_w_gate, moe_w_up, moe_w_down, final_norm):
    d = D_MODEL
    bp, sp, _ = x_prompt.shape
    bs, ss, _ = x_sample.shape
    n_p, n_s = bp * sp, bs * ss
    n = n_p + n_s
    groups = [(0, bp, sp), (n_p, bs, ss)]
    x = jnp.concatenate([x_prompt.reshape(n_p, d), x_sample.reshape(n_s, d)], axis=0)
    pos = jnp.concatenate([jnp.tile(jnp.arange(sp), bp), jnp.tile(jnp.arange(ss), bs)])
    row = lambda v: v.reshape(1, -1).astype(_F32)

    w_in0_p = jnp.pad(w_in0[0], ((0, 0), (0, IN0_PAD - IN0_WIDTH))).astype(_BF16)
    u0 = _rms_matmul(x, row(norm_mix0[0]), w_in0_p, tn=IN0_PAD)

    t_pool = _tile(math.gcd(sp, ss), 512)
    tile_rows = jnp.arange(n // t_pool) * t_pool
    tile_pos0 = pos[tile_rows].astype(jnp.int32)
    tile_slen = jnp.where(tile_rows < n_p, sp, ss).astype(jnp.int32)
    a = _pool_mixer(u0, pool_w[0].astype(_BF16), row(pool_scale[0]), tile_pos0, tile_slen, t_pool)

    wq = w_q_up[0].reshape(MLA_Q_LORA, MLA_HEADS, MLA_NOPE + MLA_ROPE)
    wq = jnp.concatenate([wq[:, :, :MLA_NOPE].reshape(MLA_Q_LORA, -1),
                          wq[:, :, MLA_NOPE:].reshape(MLA_Q_LORA, -1)], axis=1).astype(_BF16)
    wkv = w_kv_up[0].reshape(MLA_KV_LORA, MLA_HEADS, MLA_NOPE + MLA_V)
    wkv = jnp.concatenate([wkv[:, :, :MLA_NOPE].reshape(MLA_KV_LORA, -1),
                           wkv[:, :, MLA_NOPE:].reshape(MLA_KV_LORA, -1)], axis=1).astype(_BF16)
    cos_m, sin_m = _rope_tables(pos, MLA_ROPE)
    cos_m = jnp.concatenate([cos_m, cos_m], axis=-1)
    sin_m = jnp.concatenate([sin_m, sin_m], axis=-1)
    q0, k0, v0 = _mla_prep(u0, row(q_a_norm[0]), row(kv_a_norm[0]), wq, wkv, cos_m, sin_m)
    b = None
    for gi, (row0, n_seq, seq) in enumerate(groups):
        b = _flash(q0, k0, v0, b, row0=row0, n_seq=n_seq, seq=seq, heads=MLA_HEADS, kv_group=1,
                   dq=MLA_QK_PAD, dv=MLA_V, q_col0=0, k_col0=0, v_col0=0, tq=_tile(seq, FLASH_TQ),
                   tk=_tile(seq, FLASH_TK), name=f"mla_attn_{gi}")
    x = _out_proj(a, b, w_out0[0].astype(_BF16), x)
    x = _ffn(x, row(norm_ffn0[0]), ffn_w_gate[0].astype(_BF16), ffn_w_up[0].astype(_BF16),
             ffn_w_down[0].astype(_BF16))

    lambda_init = 0.8 - 0.6 * math.exp(-0.3 * 1)
    cw = DIFF_HEADS * 2 * DIFF_HD
    col_scale = jnp.concatenate([jnp.full((cw,), DIFF_HD ** -0.5 * LOG2E, _F32),
                                 jnp.ones((w_in1.shape[2] - cw,), _F32)])
    u1 = _rms_matmul(x, row(norm_mix1[0]), (w_in1[0] * col_scale[None, :]).astype(_BF16), tn=1536)

    c = None
    for gi, (row0, n_seq, seq) in enumerate(groups):
        tq, tk = _tile(seq, DIFF_TQ), _tile(seq, FLASH_TK)
        diff_args = (_bias_tiles(rel_bias, tq, min(tk, FLASH_KH)), row(lambda_q1[0]), row(lambda_k1[0]),
                     row(lambda_q2[0]), row(lambda_k2[0]), row(diff_subln[0]))
        c = _flash(u1, u1, u1, c, row0=row0, n_seq=n_seq, seq=seq, heads=DIFF_HEADS, kv_group=1,
                   dq=2 * DIFF_HD, dv=2 * DIFF_HD, q_col0=0, k_col0=DIFF_HEADS, v_col0=2 * DIFF_HEADS,
                   tq=tq, tk=tk, diff_args=diff_args, lambda_init=lambda_init, name=f"diff_attn_{gi}")

    cr, sr = _rope_tables(pos // GRID_W, AXIAL_DIM)
    cc, sc = _rope_tables(pos % GRID_W, AXIAL_DIM)
    cos_a = jnp.concatenate([cr, cc], axis=-1)
    sin_a = jnp.concatenate([sr, sc], axis=-1)
    qg, kg = _gqa_prep(u1, row(gqa_q_norm[0]), row(gqa_k_norm[0]), cos_a, sin_a)
    v_col0 = (3 * cw + (GQA_HEADS + GQA_KV) * GQA_HD) // GQA_HD
    dd = None
    for gi, (row0, n_seq, seq) in enumerate(groups):
        dd = _flash(qg, kg, u1, dd, row0=row0, n_seq=n_seq, seq=seq, heads=GQA_HEADS, kv_group=GQA_GROUP,
                    dq=GQA_HD, dv=GQA_HD, q_col0=0, k_col0=0, v_col0=v_col0, tq=_tile(seq, FLASH_TQ),
                    tk=_tile(seq, FLASH_TK), name=f"gqa_attn_{gi}")
    x = _out_proj(c, dd, w_out1[0].astype(_BF16), x)

    w_r = jnp.pad(router_w[0], ((0, 0), (0, LANES - N_EXPERTS)))
    b_r = jnp.pad(router_b[0], (0, LANES - N_EXPERTS)).reshape(1, LANES)
    h2, route = _router(x, row(norm_ffn1[0]), w_r, b_r)
    experts = route[:, :TOP_K].astype(jnp.int32)
    tm_moe = _tile(n * TOP_K, 1024)
    n_tiles = n * TOP_K // tm_moe + N_EXPERTS + 1
    slot, row_token, tile_expert, n_used = _moe_plan(experts, tm_moe, n_tiles)
    tc = _tile(n, 256)
    pos3 = slot.reshape(n // tc, 1, tc * TOP_K)
    ys = _moe_ffn(h2, row_token.reshape(n_tiles, 1, tm_moe), tile_expert, n_used,
                  moe_w_gate[0].astype(_BF16), moe_w_up[0].astype(_BF16), moe_w_down[0].astype(_BF16))
    y = _combine(pos3, route, x, row(final_norm), ys)
    return (y[:n_p].reshape(bp, sp, d), y[n_p:].reshape(bs, ss, d))
```

```python
import functools
import math

import jax
import jax.numpy as jnp
from jax import lax
from jax.experimental import pallas as pl
from jax.experimental.pallas import tpu as pltpu

D_MODEL = 2048
RMS_EPS = 1e-6
ROPE_THETA = 10000.0
GRID_W = 64
POOL_WINDOWS = (2, 4, 8, 16)
POOL_GROUP = D_MODEL // 8
POOL_WIDTH = POOL_GROUP * len(POOL_WINDOWS)
MLA_HEADS = 8
MLA_Q_LORA = D_MODEL // 4
MLA_KV_LORA = D_MODEL // 8
MLA_NOPE = 128
MLA_ROPE = 64
MLA_V = 128
MLA_QK_PAD = 256
IN0_WIDTH = POOL_WIDTH + MLA_Q_LORA + MLA_KV_LORA + MLA_ROPE
IN0_PAD = 1920
DIFF_HEADS = 8
DIFF_HD = 64
GQA_HEADS = 8
GQA_KV = 2
GQA_GROUP = GQA_HEADS // GQA_KV
GQA_HD = 128
AXIAL_DIM = GQA_HD // 2
REL_BUCKETS = 32
REL_MAX_DIST = 128
N_EXPERTS = 8
TOP_K = 2
LANES = 128
HALO = 16
FLASH_SUB = 256
FLASH_TQ = 2048
FLASH_TK = 2048
FLASH_KH = 1024
DIFF_TQ = 512
LOG2E = math.log2(math.e)

_F32 = jnp.float32
_BF16 = jnp.bfloat16
_VMEM_LIMIT = 56 * 1024 * 1024
_NEG = -1e30


def _cparams(*sem):
    return pltpu.CompilerParams(dimension_semantics=sem, vmem_limit_bytes=_VMEM_LIMIT)


def _tile(n, pref):
    t = pref
    while n % t:
        t //= 2
    return t


def _rms(x, g):
    ms = jnp.mean(x * x, axis=-1, keepdims=True)
    return x * lax.rsqrt(ms + RMS_EPS) * g


def _rope_partner(x):
    lane = lax.broadcasted_iota(jnp.int32, x.shape, 1)
    first = (lane % 64) < 32
    return jnp.where(first, pltpu.roll(x, 96, 1), pltpu.roll(x, 32, 1))


def _row_parts(parts, tm, col_block, col_of):
    def index_map(first, nt):
        return lambda i, j: (jnp.clip(i - first, 0, nt - 1), col_of(j))

    specs, first = [], 0
    for part in parts:
        assert part.shape[0] % tm == 0
        nt = part.shape[0] // tm
        specs.append(pl.BlockSpec((tm, col_block), index_map(first, nt)))
        first += nt
    return specs


def _select_part(refs, part_tiles):
    val = refs[-1][...]
    end = sum(part_tiles) - part_tiles[-1]
    for ref, nt in zip(reversed(refs[:-1]), reversed(part_tiles[:-1])):
        val = jnp.where(pl.program_id(0) < end, ref[...], val)
        end -= nt
    return val


def _rms_matmul_kernel(*refs, part_tiles):
    x_refs = refs[:len(part_tiles)]
    g_ref, w_ref, o_ref, h_sc = refs[len(part_tiles):]

    @pl.when(pl.program_id(1) == 0)
    def _():
        h_sc[...] = _rms(_select_part(x_refs, part_tiles), g_ref[...]).astype(_BF16)

    o_ref[...] = jnp.dot(h_sc[...], w_ref[...], preferred_element_type=_F32).astype(o_ref.dtype)


def _rms_matmul(x_parts, g, w, tn, tm_pref=1024):
    d = x_parts[0].shape[1]
    n = sum(p.shape[0] for p in x_parts)
    width = w.shape[1]
    tm = _tile(math.gcd(*[p.shape[0] for p in x_parts]), tm_pref)
    part_tiles = tuple(p.shape[0] // tm for p in x_parts)
    return pl.pallas_call(
        functools.partial(_rms_matmul_kernel, part_tiles=part_tiles),
        grid=(n // tm, width // tn),
        in_specs=_row_parts(x_parts, tm, d, lambda j: 0) + [
            pl.BlockSpec((1, d), lambda i, j: (0, 0)),
            pl.BlockSpec((d, tn), lambda i, j: (0, j)),
        ],
        out_specs=pl.BlockSpec((tm, tn), lambda i, j: (i, j)),
        out_shape=jax.ShapeDtypeStruct((n, width), _BF16),
        scratch_shapes=[pltpu.VMEM((tm, d), _BF16)],
        compiler_params=_cparams("parallel", "arbitrary"),
        name="rms_matmul",
    )(*x_parts, g, w)


def _out_proj_kernel(a_ref, b_ref, w_ref, *refs, part_tiles):
    x_refs, o_ref = refs[:-1], refs[-1]
    ka = a_ref.shape[1]
    acc = jnp.dot(a_ref[...], w_ref[:ka, :], preferred_element_type=_F32)
    acc = acc + jnp.dot(b_ref[...], w_ref[ka:, :], preferred_element_type=_F32)
    o_ref[...] = _select_part(x_refs, part_tiles) + acc


def _out_proj(a, b, w, x_parts):
    d = x_parts[0].shape[1]
    n = sum(p.shape[0] for p in x_parts)
    tm = _tile(math.gcd(*[p.shape[0] for p in x_parts]), 512)
    part_tiles = tuple(p.shape[0] // tm for p in x_parts)
    tn = d
    return pl.pallas_call(
        functools.partial(_out_proj_kernel, part_tiles=part_tiles),
        grid=(n // tm, d // tn),
        in_specs=[
            pl.BlockSpec((tm, a.shape[1]), lambda i, j: (i, 0)),
            pl.BlockSpec((tm, b.shape[1]), lambda i, j: (i, 0)),
            pl.BlockSpec((w.shape[0], tn), lambda i, j: (0, j)),
        ] + _row_parts(x_parts, tm, tn, lambda j: j),
        out_specs=pl.BlockSpec((tm, tn), lambda i, j: (i, j)),
        out_shape=jax.ShapeDtypeStruct((n, d), _F32),
        compiler_params=_cparams("parallel", "arbitrary"),
        name="out_proj",
    )(a, b, w, *x_parts)


def _swiglu_acc(h, wg, wu, wd):
    gate = jnp.dot(h, wg, preferred_element_type=_F32)
    up = jnp.dot(h, wu, preferred_element_type=_F32)
    act = (gate / (1.0 + jnp.exp(-gate)) * up).astype(_BF16)
    return jnp.dot(act, wd, preferred_element_type=_F32)


def _ffn_kernel(x_ref, g_ref, wg_ref, wu_ref, wd_ref, o_ref, h_sc):
    @pl.when(pl.program_id(1) == 0)
    def _():
        x = x_ref[...]
        h_sc[...] = _rms(x, g_ref[...]).astype(_BF16)
        o_ref[...] = x

    o_ref[...] += _swiglu_acc(h_sc[...], wg_ref[...], wu_ref[...], wd_ref[...])


def _ffn(x, g, wg, wu, wd):
    n, d = x.shape
    f = wg.shape[1]
    tm = _tile(n, 512)
    tf = 512
    return pl.pallas_call(
        _ffn_kernel,
        grid=(n // tm, f // tf),
        in_specs=[
            pl.BlockSpec((tm, d), lambda i, j: (i, 0)),
            pl.BlockSpec((1, d), lambda i, j: (0, 0)),
            pl.BlockSpec((d, tf), lambda i, j: (0, j)),
            pl.BlockSpec((d, tf), lambda i, j: (0, j)),
            pl.BlockSpec((tf, d), lambda i, j: (j, 0)),
        ],
        out_specs=pl.BlockSpec((tm, d), lambda i, j: (i, 0)),
        out_shape=jax.ShapeDtypeStruct((n, d), _F32),
        scratch_shapes=[pltpu.VMEM((tm, d), _BF16)],
        compiler_params=_cparams("parallel", "arbitrary"),
        name="ffn_swiglu",
    )(x, g, wg, wu, wd)


def _pool_kernel(pos0_ref, slen_ref, prev_ref, cur_ref, next_ref, w_ref, scale_ref, o_ref, buf_sc):
    i = pl.program_id(0)
    t = cur_ref.shape[0]
    pos0 = pos0_ref[i]
    slen = slen_ref[i]
    cur = cur_ref[...].astype(_F32)
    buf_sc[0:HALO, :] = jnp.where(pos0 > 0, prev_ref[...].astype(_F32), 0.0)
    buf_sc[HALO:HALO + t, :] = cur
    buf_sc[HALO + t:, :] = jnp.where(pos0 + t < slen, next_ref[...].astype(_F32), 0.0)
    tpos = pos0 + lax.broadcasted_iota(jnp.int32, (t, 1), 0)
    for g, win in enumerate(POOL_WINDOWS):
        c0, c1 = g * POOL_GROUP, (g + 1) * POOL_GROUP
        half = win // 2
        acc = buf_sc[HALO - half:HALO - half + t, c0:c1]
        for dlt in range(-half + 1, half):
            acc = acc + buf_sc[HALO + dlt:HALO + dlt + t, c0:c1]
        count = (jnp.minimum(tpos + half, slen) - jnp.maximum(tpos - half, 0)).astype(_F32)
        p = (acc / count - cur[:, c0:c1]).astype(_BF16)
        y = jnp.dot(p, w_ref[g], preferred_element_type=_F32)
        o_ref[:, c0:c1] = (y * scale_ref[:, c0:c1]).astype(o_ref.dtype)


def _pool_mixer(u, pool_w, pool_scale, tile_pos0, tile_slen, t):
    n = u.shape[0]
    hb = t // HALO
    last = n // HALO - 1
    grid_spec = pltpu.PrefetchScalarGridSpec(
        num_scalar_prefetch=2,
        grid=(n // t,),
        in_specs=[
            pl.BlockSpec((HALO, POOL_WIDTH), lambda i, p, s: (jnp.maximum(i * hb - 1, 0), 0)),
            pl.BlockSpec((t, POOL_WIDTH), lambda i, p, s: (i, 0)),
            pl.BlockSpec((HALO, POOL_WIDTH), lambda i, p, s: (jnp.minimum((i + 1) * hb, last), 0)),
            pl.BlockSpec((len(POOL_WINDOWS), POOL_GROUP, POOL_GROUP), lambda i, p, s: (0, 0, 0)),
            pl.BlockSpec((1, POOL_WIDTH), lambda i, p, s: (0, 0)),
        ],
        out_specs=pl.BlockSpec((t, POOL_WIDTH), lambda i, p, s: (i, 0)),
        scratch_shapes=[pltpu.VMEM((t + 2 * HALO, POOL_WIDTH), _F32)],
    )
    return pl.pallas_call(
        _pool_kernel,
        grid_spec=grid_spec,
        out_shape=jax.ShapeDtypeStruct((n, POOL_WIDTH), _BF16),
        compiler_params=_cparams("arbitrary"),
        name="pool_mixer",
    )(tile_pos0, tile_slen, u, u, u, pool_w, pool_scale)


def _mla_prep_kernel(ql_ref, kvl_ref, kr_ref, gq_ref, gkv_ref, wq_ref, wkv_ref, cos_ref, sin_ref,
                     q_ref, k_ref, v_ref):
    scale = (MLA_NOPE + MLA_ROPE) ** -0.5 * LOG2E
    cos = cos_ref[...]
    sin = sin_ref[...]
    lane = lax.broadcasted_iota(jnp.int32, cos.shape, 1)
    low = lane < MLA_ROPE

    qn = _rms(ql_ref[...].astype(_F32), gq_ref[...]).astype(_BF16)
    q = jnp.dot(qn, wq_ref[...], preferred_element_type=_F32)
    kvn = _rms(kvl_ref[...].astype(_F32), gkv_ref[...]).astype(_BF16)
    kv = jnp.dot(kvn, wkv_ref[...], preferred_element_type=_F32)
    nope_w = MLA_HEADS * MLA_NOPE
    v_ref[...] = kv[:, nope_w:].astype(v_ref.dtype)

    kr = kr_ref[...].astype(_F32)
    k_pe = jnp.where(low, kr * cos + _rope_partner(kr) * sin, 0.0).astype(k_ref.dtype)
    for pair in range(MLA_HEADS // 2):
        pe = q[:, nope_w + pair * LANES:nope_w + (pair + 1) * LANES]
        roped = (pe * cos + _rope_partner(pe) * sin) * scale
        for sub in range(2):
            h = 2 * pair + sub
            pe_h = roped if sub == 0 else pltpu.roll(roped, MLA_ROPE, 1)
            base = h * MLA_QK_PAD
            q_ref[:, base:base + MLA_NOPE] = (q[:, h * MLA_NOPE:(h + 1) * MLA_NOPE] * scale).astype(q_ref.dtype)
            q_ref[:, base + MLA_NOPE:base + MLA_QK_PAD] = jnp.where(low, pe_h, 0.0).astype(q_ref.dtype)
            k_ref[:, base:base + MLA_NOPE] = kv[:, h * MLA_NOPE:(h + 1) * MLA_NOPE].astype(k_ref.dtype)
            k_ref[:, base + MLA_NOPE:base + MLA_QK_PAD] = k_pe


def _mla_prep(u, gq, gkv, wq, wkv, cos_t, sin_t):
    n = u.shape[0]
    tm = _tile(n, 512)
    qk_w = MLA_HEADS * MLA_QK_PAD
    v_w = MLA_HEADS * MLA_V
    return pl.pallas_call(
        _mla_prep_kernel,
        grid=(n // tm,),
        in_specs=[
            pl.BlockSpec((tm, MLA_Q_LORA), lambda i: (i, POOL_WIDTH // MLA_Q_LORA)),
            pl.BlockSpec((tm, MLA_KV_LORA), lambda i: (i, (POOL_WIDTH + MLA_Q_LORA) // MLA_KV_LORA)),
            pl.BlockSpec((tm, LANES), lambda i: (i, (POOL_WIDTH + MLA_Q_LORA + MLA_KV_LORA) // LANES)),
            pl.BlockSpec((1, MLA_Q_LORA), lambda i: (0, 0)),
            pl.BlockSpec((1, MLA_KV_LORA), lambda i: (0, 0)),
            pl.BlockSpec(wq.shape, lambda i: (0, 0)),
            pl.BlockSpec(wkv.shape, lambda i: (0, 0)),
            pl.BlockSpec((tm, LANES), lambda i: (i, 0)),
            pl.BlockSpec((tm, LANES), lambda i: (i, 0)),
        ],
        out_specs=[
            pl.BlockSpec((tm, qk_w), lambda i: (i, 0)),
            pl.BlockSpec((tm, qk_w), lambda i: (i, 0)),
            pl.BlockSpec((tm, v_w), lambda i: (i, 0)),
        ],
        out_shape=[
            jax.ShapeDtypeStruct((n, qk_w), _BF16),
            jax.ShapeDtypeStruct((n, qk_w), _BF16),
            jax.ShapeDtypeStruct((n, v_w), _BF16),
        ],
        compiler_params=_cparams("parallel"),
        name="mla_prep",
    )(u, u, u, gq, gkv, wq, wkv, cos_t, sin_t)


def _gqa_prep_kernel(dq_ref, dk_ref, gq_ref, gk_ref, cos_ref, sin_ref, q_ref, k_ref):
    scale = GQA_HD ** -0.5 * LOG2E
    cos = cos_ref[...]
    sin = sin_ref[...]

    def norm_rope(x, g):
        xn = _rms(x.astype(_F32), g)
        return xn * cos + _rope_partner(xn) * sin

    for h in range(GQA_HEADS):
        sl = slice(h * GQA_HD, (h + 1) * GQA_HD)
        q_ref[:, sl] = (norm_rope(dq_ref[:, sl], gq_ref[...]) * scale).astype(q_ref.dtype)
    for h in range(GQA_KV):
        sl = slice(h * GQA_HD, (h + 1) * GQA_HD)
        k_ref[:, sl] = norm_rope(dk_ref[:, sl], gk_ref[...]).astype(k_ref.dtype)


def _gqa_prep(u, gq, gk, cos_t, sin_t):
    n = u.shape[0]
    tm = _tile(n, 512)
    q_w = GQA_HEADS * GQA_HD
    k_w = GQA_KV * GQA_HD
    dq_off = 3 * DIFF_HEADS * 2 * DIFF_HD
    return pl.pallas_call(
        _gqa_prep_kernel,
        grid=(n // tm,),
        in_specs=[
            pl.BlockSpec((tm, q_w), lambda i: (i, dq_off // q_w)),
            pl.BlockSpec((tm, k_w), lambda i: (i, (dq_off + q_w) // k_w)),
            pl.BlockSpec((1, GQA_HD), lambda i: (0, 0)),
            pl.BlockSpec((1, GQA_HD), lambda i: (0, 0)),
            pl.BlockSpec((tm, LANES), lambda i: (i, 0)),
            pl.BlockSpec((tm, LANES), lambda i: (i, 0)),
        ],
        out_specs=[
            pl.BlockSpec((tm, q_w), lambda i: (i, 0)),
            pl.BlockSpec((tm, k_w), lambda i: (i, 0)),
        ],
        out_shape=[
            jax.ShapeDtypeStruct((n, q_w), _BF16),
            jax.ShapeDtypeStruct((n, k_w), _BF16),
        ],
        compiler_params=_cparams("parallel"),
        name="gqa_prep",
    )(u, u, gq, gk, cos_t, sin_t)


def _flash_step(q_src, k_ref, v_ref, bias_refs, m_sc, l_sc, acc_sc, tq):
    tk = k_ref.shape[0]
    kh = min(tk, FLASH_KH)
    n_chunks = kh // LANES
    starts = range(0, m_sc.shape[0], FLASH_SUB)

    def score(half, r0):
        k = k_ref[half * kh:(half + 1) * kh, :]
        return lax.dot_general(q_src[r0:r0 + FLASH_SUB, :], k, (((1,), (1,)), ((), ())),
                               preferred_element_type=_F32)

    def softmax_pv(half, r0, s):
        sl = slice(r0, r0 + FLASH_SUB)
        chunks = [s[:, c * LANES:(c + 1) * LANES] for c in range(n_chunks)]
        if bias_refs is not None:
            b0 = r0 % tq
            chunks = [bias_refs[half][b0:b0 + FLASH_SUB, c * LANES:(c + 1) * LANES] + ch
                      for c, ch in enumerate(chunks)]
        m_prev = m_sc[sl, :]
        row_max = jnp.max(functools.reduce(jnp.maximum, chunks), axis=-1, keepdims=True)
        m_new = jnp.maximum(m_prev, row_max)
        alpha = jnp.exp2(m_prev - m_new)
        ps = [jnp.exp2(ch - m_new) for ch in chunks]
        l_sc[sl, :] = alpha * l_sc[sl, :] + functools.reduce(jnp.add, ps)
        p = jnp.concatenate(ps, axis=1).astype(_BF16)
        v = v_ref[half * kh:(half + 1) * kh, :]
        acc_sc[sl, :] = alpha * acc_sc[sl, :] + jnp.dot(p, v, preferred_element_type=_F32)
        m_sc[sl, :] = m_new

    scores = [score(0, r0) for r0 in starts]
    for half in range(tk // kh):
        nxt = []
        for r0, s in zip(starts, scores):
            softmax_pv(half, r0, s)
            if (half + 1) * kh < tk:
                nxt.append(score(half + 1, r0))
        scores = nxt


def _flash_kernel(*refs, diff, n_halves, lambda_init):
    if diff:
        q_ref, k_ref, v_ref = refs[:3]
        bias_refs = refs[3:3 + n_halves]
        (lq1_ref, lk1_ref, lq2_ref, lk2_ref, sub_ref, _, o_ref,
         q_sc, m_sc, l_sc, acc_sc) = refs[3 + n_halves:]
    else:
        q_ref, k_ref, v_ref, _, o_ref, m_sc, l_sc, acc_sc = refs
    ki = pl.program_id(3)
    tq = q_ref.shape[0]

    @pl.when(ki == 0)
    def _():
        m_sc[...] = jnp.full_like(m_sc, _NEG)
        l_sc[...] = jnp.zeros_like(l_sc)
        acc_sc[...] = jnp.zeros_like(acc_sc)
        if diff:
            q = q_ref[...]
            lane = lax.broadcasted_iota(jnp.int32, q.shape, 1)
            zero = jnp.zeros_like(q)
            q_sc[0:tq, :] = jnp.where(lane < DIFF_HD, q, zero)
            q_sc[tq:, :] = jnp.where(lane >= DIFF_HD, q, zero)

    if diff:
        _flash_step(q_sc, k_ref, v_ref, bias_refs, m_sc, l_sc, acc_sc, tq)
    else:
        _flash_step(q_ref, k_ref, v_ref, None, m_sc, l_sc, acc_sc, tq)

    @pl.when(ki == pl.num_programs(3) - 1)
    def _():
        o = acc_sc[...] / jnp.sum(l_sc[...], axis=-1, keepdims=True)
        if diff:
            lam = (jnp.exp(jnp.sum(lq1_ref[...] * lk1_ref[...], axis=-1, keepdims=True))
                   - jnp.exp(jnp.sum(lq2_ref[...] * lk2_ref[...], axis=-1, keepdims=True))
                   + lambda_init)
            o = o[0:tq, :] - lam * o[tq:, :]
            o = _rms(o, sub_ref[...]) * (1.0 - lambda_init)
        o_ref[...] = o.astype(o_ref.dtype)


def _flash(q_arr, k_arr, v_arr, prev_out, *, row0, n_seq, seq, heads, kv_group, dq, dv,
           q_col0, k_col0, v_col0, tq, tk, diff_args=None, lambda_init=0.0, name):
    n = q_arr.shape[0]
    nq, nk = seq // tq, seq // tk
    diff = diff_args is not None
    rows = 2 * tq if diff else tq
    assert rows % FLASH_SUB == 0 and tq % FLASH_SUB == 0

    def q_map(b, h, qi, ki):
        return (row0 // tq + b * nq + qi, q_col0 + h)

    def k_map(b, h, qi, ki):
        return (row0 // tk + b * nk + ki, k_col0 + h // kv_group)

    def v_map(b, h, qi, ki):
        return (row0 // tk + b * nk + ki, v_col0 + h // kv_group)

    def o_map(b, h, qi, ki):
        return (row0 // tq + b * nq + qi, h)

    in_specs = [pl.BlockSpec((tq, dq), q_map), pl.BlockSpec((tk, dq), k_map), pl.BlockSpec((tk, dv), v_map)]
    args = [q_arr, k_arr, v_arr]
    scratch = []
    kh = min(tk, FLASH_KH)
    n_halves = tk // kh
    if diff:
        assert kh % tq == 0 and tq >= REL_MAX_DIST
        ratio = kh // tq
        bias, lq1, lk1, lq2, lk2, sub = diff_args
        assert bias.shape[1:] == (ratio + 4, tq, kh)

        def bias_map(half):
            def index_map(b, h, qi, ki):
                off = ratio * (n_halves * ki + half) - qi
                return (h, jnp.clip(off, -ratio - 1, 2) + ratio + 1, 0, 0)
            return index_map

        for half in range(n_halves):
            in_specs.append(pl.BlockSpec((None, None, tq, kh), bias_map(half)))
            args.append(bias)
        in_specs += [pl.BlockSpec((1, DIFF_HD), lambda b, h, qi, ki: (0, 0))] * 4
        in_specs.append(pl.BlockSpec((1, 2 * DIFF_HD), lambda b, h, qi, ki: (0, 0)))
        args += [lq1, lk1, lq2, lk2, sub]
        scratch.append(pltpu.VMEM((rows, dq), _BF16))
    out_shape = jax.ShapeDtypeStruct((n, heads * dv), _BF16)
    if prev_out is None:
        prev_out = jnp.zeros(out_shape.shape, out_shape.dtype)
    aliases = {len(args): 0}
    in_specs.append(pl.BlockSpec(memory_space=pl.ANY))
    args.append(prev_out)
    assert dv == LANES
    scratch += [pltpu.VMEM((rows, LANES), _F32), pltpu.VMEM((rows, LANES), _F32), pltpu.VMEM((rows, dv), _F32)]
    return pl.pallas_call(
        functools.partial(_flash_kernel, diff=diff, n_halves=n_halves, lambda_init=lambda_init),
        grid=(n_seq, heads, nq, nk),
        in_specs=in_specs,
        out_specs=pl.BlockSpec((tq, dv), o_map),
        out_shape=out_shape,
        scratch_shapes=scratch,
        input_output_aliases=aliases,
        compiler_params=_cparams("parallel", "parallel", "parallel", "arbitrary"),
        name=name,
    )(*args)


def _router_kernel(x_ref, g_ref, w_ref, b_ref, h_ref, route_ref):
    h = _rms(x_ref[...], g_ref[...])
    h_ref[...] = h
    logits = jnp.dot(h, w_ref[...], preferred_element_type=_F32, precision=lax.Precision.HIGHEST) + b_ref[...]
    lane = lax.broadcasted_iota(jnp.int32, logits.shape, 1)
    logits = jnp.where(lane < N_EXPERTS, logits, _NEG)
    v1 = jnp.max(logits, axis=-1, keepdims=True)
    i1 = jnp.min(jnp.where(logits == v1, lane, LANES), axis=-1, keepdims=True)
    rest = jnp.where(lane == i1, _NEG, logits)
    v2 = jnp.max(rest, axis=-1, keepdims=True)
    i2 = jnp.min(jnp.where(rest == v2, lane, LANES), axis=-1, keepdims=True)
    g1 = 1.0 / (1.0 + jnp.exp(v2 - v1))
    g2 = 1.0 - g1
    route = jnp.where(lane == 0, i1.astype(_F32),
                      jnp.where(lane == 1, i2.astype(_F32),
                                jnp.where(lane == 2, g1, jnp.where(lane == 3, g2, 0.0))))
    route_ref[...] = route


def _router(x, g, w_pad, b_pad):
    n, d = x.shape
    tm = _tile(n, 512)
    return pl.pallas_call(
        _router_kernel,
        grid=(n // tm,),
        in_specs=[
            pl.BlockSpec((tm, d), lambda i: (i, 0)),
            pl.BlockSpec((1, d), lambda i: (0, 0)),
            pl.BlockSpec((d, LANES), lambda i: (0, 0)),
            pl.BlockSpec((1, LANES), lambda i: (0, 0)),
        ],
        out_specs=[pl.BlockSpec((tm, d), lambda i: (i, 0)), pl.BlockSpec((tm, LANES), lambda i: (i, 0))],
        out_shape=[jax.ShapeDtypeStruct((n, d), _F32), jax.ShapeDtypeStruct((n, LANES), _F32)],
        compiler_params=_cparams("parallel"),
        name="moe_router",
    )(x, g, w_pad, b_pad)


def _gather_rows(src_hbm, idx_ref, idx0, stride, dst, n_rows, last, sem, *, wait):
    def step(i, carry):
        row = idx_ref[0, 0, idx0 + stride * jnp.minimum(i, last)]
        copy = pltpu.make_async_copy(src_hbm.at[pl.ds(row, 1), :], dst.at[pl.ds(i, 1), :], sem)
        if wait:
            copy.wait()
        else:
            copy.start()
        return carry

    lax.fori_loop(0, n_rows, step, 0, unroll=8)


def _moe_ffn_kernel(te_ref, nused_ref, tok_ref, next_tok_ref, h_ref, wg_ref, wu_ref, wd_ref, ys_ref,
                    xf_sc, xb_sc, sem):
    r = pl.program_id(0)
    j = pl.program_id(1)
    n_used = nused_ref[0]
    tm = xb_sc.shape[0]
    n_stage = xf_sc.shape[0]
    per_step = n_stage // pl.num_programs(1)

    @pl.when(j == 0)
    def _():
        ys_ref[...] = jnp.zeros_like(ys_ref)

        @pl.when(r == 0)
        def _():
            _gather_rows(h_ref, tok_ref, 0, 1, xf_sc, n_stage, tm - 1, sem, wait=False)

        @pl.when(r <= n_used)
        def _():
            _gather_rows(h_ref, tok_ref, 0, 1, xf_sc, n_stage, tm - 1, sem, wait=True)
            xb_sc[...] = xf_sc[0:tm, :].astype(_BF16)

    @pl.when(r < n_used)
    def _():
        for t in range(per_step):
            i = j * per_step + t
            row = next_tok_ref[0, 0, jnp.minimum(i, tm - 1)]
            pltpu.make_async_copy(h_ref.at[pl.ds(row, 1), :], xf_sc.at[pl.ds(i, 1), :], sem).start()
        ys_ref[...] += _swiglu_acc(xb_sc[...], wg_ref[...], wu_ref[...], wd_ref[...])


def _moe_ffn(h, row_token3, tile_expert, n_used, wg, wu, wd):
    n_tiles, _, tm = row_token3.shape
    d = h.shape[1]
    f = wg.shape[2]
    tf = 512
    n_steps = f // tf
    n_stage = -(-tm // n_steps) * n_steps
    grid_spec = pltpu.PrefetchScalarGridSpec(
        num_scalar_prefetch=2,
        grid=(n_tiles, n_steps),
        in_specs=[
            pl.BlockSpec((1, 1, tm), lambda r, j, te, nu: (r, 0, 0), memory_space=pltpu.SMEM),
            pl.BlockSpec((1, 1, tm), lambda r, j, te, nu: (jnp.minimum(r + 1, n_tiles - 1), 0, 0),
                         memory_space=pltpu.SMEM),
            pl.BlockSpec(memory_space=pl.ANY),
            pl.BlockSpec((None, d, tf), lambda r, j, te, nu: (te[r], 0, j)),
            pl.BlockSpec((None, d, tf), lambda r, j, te, nu: (te[r], 0, j)),
            pl.BlockSpec((None, tf, d), lambda r, j, te, nu: (te[r], j, 0)),
        ],
        out_specs=pl.BlockSpec((tm, d), lambda r, j, te, nu: (r, 0)),
        scratch_shapes=[pltpu.VMEM((n_stage, d), _F32), pltpu.VMEM((tm, d), _BF16),
                        pltpu.SemaphoreType.DMA(())],
    )
    return pl.pallas_call(
        _moe_ffn_kernel,
        grid_spec=grid_spec,
        out_shape=jax.ShapeDtypeStruct((n_tiles * tm, d), _F32),
        compiler_params=_cparams("arbitrary", "arbitrary"),
        name="moe_expert_ffn",
    )(tile_expert, n_used, row_token3, row_token3, h, wg, wu, wd)


def _combine_kernel(pos_ref, route_ref, x_ref, g_ref, ys_ref, o_first_ref, o_rest_ref, buf, sem, *, n_first):
    tc = x_ref.shape[0]
    for wait in (False, True):
        for k in range(TOP_K):
            _gather_rows(ys_ref, pos_ref, k, TOP_K, buf.at[k], tc, tc - 1, sem.at[k], wait=wait)
    route = route_ref[...]
    y = x_ref[...] + route[:, 2:3] * buf[0] + route[:, 3:4] * buf[1]
    out = _rms(y, g_ref[...])

    @pl.when(pl.program_id(0) < n_first)
    def _():
        o_first_ref[...] = out

    @pl.when(pl.program_id(0) >= n_first)
    def _():
        o_rest_ref[...] = out


def _combine(pos3, route, x, g, ys, n_first_rows):
    n, d = x.shape
    n_chunks, _, width = pos3.shape
    tc = width // TOP_K
    assert n_first_rows % tc == 0 and 0 < n_first_rows < n
    n_first = n_first_rows // tc
    return pl.pallas_call(
        functools.partial(_combine_kernel, n_first=n_first),
        grid=(n_chunks,),
        in_specs=[
            pl.BlockSpec((1, 1, width), lambda c: (c, 0, 0), memory_space=pltpu.SMEM),
            pl.BlockSpec((tc, LANES), lambda c: (c, 0)),
            pl.BlockSpec((tc, d), lambda c: (c, 0)),
            pl.BlockSpec((1, d), lambda c: (0, 0)),
            pl.BlockSpec(memory_space=pl.ANY),
        ],
        out_specs=[pl.BlockSpec((tc, d), lambda c: (jnp.minimum(c, n_first - 1), 0)),
                   pl.BlockSpec((tc, d), lambda c: (jnp.maximum(c - n_first, 0), 0))],
        out_shape=[jax.ShapeDtypeStruct((n_first_rows, d), _F32),
                   jax.ShapeDtypeStruct((n - n_first_rows, d), _F32)],
        scratch_shapes=[pltpu.VMEM((TOP_K, tc, d), _F32), pltpu.SemaphoreType.DMA((TOP_K,))],
        compiler_params=_cparams("arbitrary"),
        name="moe_combine",
    )(pos3, route, x, g, ys)


def _rope_tables(pos, dim):
    inv = ROPE_THETA ** (-jnp.arange(0, dim, 2, dtype=_F32) / dim)
    ang = pos.astype(_F32)[:, None] * inv[None, :]
    c, s = jnp.cos(ang), jnp.sin(ang)
    return jnp.concatenate([c, c], axis=-1), jnp.concatenate([-s, s], axis=-1)


def _t5_bucket(rel):
    nb = REL_BUCKETS // 2
    max_exact = nb // 2
    ret = jnp.where(rel > 0, nb, 0)
    n = jnp.abs(rel)
    n_f = jnp.maximum(n, 1).astype(_F32)
    large = max_exact + (jnp.log(n_f / max_exact) / math.log(REL_MAX_DIST / max_exact)
                         * (nb - max_exact)).astype(jnp.int32)
    large = jnp.minimum(large, nb - 1)
    return ret + jnp.where(n < max_exact, n, large)


def _bias_tiles(rel_bias, tq, tk):
    ratio = tk // tq
    table = rel_bias.astype(_F32) * LOG2E
    off = jnp.arange(-ratio, 2)[:, None, None]
    rel = off * tq + jnp.arange(tk)[None, None, :] - jnp.arange(tq)[None, :, None]
    onehot = (_t5_bucket(rel)[..., None] == jnp.arange(REL_BUCKETS)).astype(_F32)
    near = jnp.einsum("dqkb,bh->hdqk", onehot, table, precision=lax.Precision.HIGHEST)
    far = table[_t5_bucket(jnp.array([-REL_MAX_DIST, REL_MAX_DIST]))].T
    far = jnp.broadcast_to(far[:, :, None, None], (far.shape[0], 2, tq, tk))
    return jnp.concatenate([far[:, :1], near, far[:, 1:]], axis=1)


def _moe_plan(experts, tm, n_tiles):
    e_flat = experts.reshape(-1)
    onehot = (e_flat[:, None] == jnp.arange(N_EXPERTS)[None, :]).astype(jnp.int32)
    csum = jnp.cumsum(onehot, axis=0)
    counts = csum[-1]
    rank = jnp.sum(onehot * (csum - 1), axis=1)
    padded = ((counts + tm - 1) // tm) * tm
    ends = jnp.cumsum(padded)
    starts = ends - padded
    pos = starts[e_flat] + rank
    tile_start = jnp.arange(n_tiles, dtype=jnp.int32) * tm
    tile_expert = jnp.minimum(jnp.sum(tile_start[:, None] >= ends[None, :], axis=1), N_EXPERTS - 1)
    n_used = (ends[-1] // tm).reshape(1)
    pair_token = jnp.arange(e_flat.shape[0], dtype=jnp.int32) // TOP_K
    row_token = jnp.zeros((n_tiles * tm,), jnp.int32).at[pos].set(pair_token, unique_indices=True)
    return pos.astype(jnp.int32), row_token, tile_expert.astype(jnp.int32), n_used.astype(jnp.int32)


def kernel(x_prompt, x_sample, rel_bias, norm_mix0, w_in0, pool_w, pool_scale, q_a_norm, w_q_up, kv_a_norm, w_kv_up, w_out0, norm_ffn0, ffn_w_gate, ffn_w_up, ffn_w_down, norm_mix1, w_in1, lambda_q1, lambda_k1, lambda_q2, lambda_k2, diff_subln, gqa_q_norm, gqa_k_norm, w_out1, norm_ffn1, router_w, router_b, moe_w_gate, moe_w_up, moe_w_down, final_norm):
    d = D_MODEL
    bp, sp, _ = x_prompt.shape
    bs, ss, _ = x_sample.shape
    n_p, n_s = bp * sp, bs * ss
    n = n_p + n_s
    groups = [(0, bp, sp), (n_p, bs, ss)]
    x_in = (x_prompt.reshape(n_p, d), x_sample.reshape(n_s, d))
    pos = jnp.concatenate([jnp.tile(jnp.arange(sp), bp), jnp.tile(jnp.arange(ss), bs)])
    row = lambda v: v.reshape(1, -1).astype(_F32)

    w_in0_p = jnp.pad(w_in0[0], ((0, 0), (0, IN0_PAD - IN0_WIDTH))).astype(_BF16)
    u0 = _rms_matmul(x_in, row(norm_mix0[0]), w_in0_p, tn=IN0_PAD, tm_pref=512)

    t_pool = _tile(math.gcd(sp, ss), 512)
    tile_rows = jnp.arange(n // t_pool) * t_pool
    tile_pos0 = pos[tile_rows].astype(jnp.int32)
    tile_slen = jnp.where(tile_rows < n_p, sp, ss).astype(jnp.int32)
    a = _pool_mixer(u0, pool_w[0].astype(_BF16), row(pool_scale[0]), tile_pos0, tile_slen, t_pool)

    wq = w_q_up[0].reshape(MLA_Q_LORA, MLA_HEADS, MLA_NOPE + MLA_ROPE)
    wq = jnp.concatenate([wq[:, :, :MLA_NOPE].reshape(MLA_Q_LORA, -1),
                          wq[:, :, MLA_NOPE:].reshape(MLA_Q_LORA, -1)], axis=1).astype(_BF16)
    wkv = w_kv_up[0].reshape(MLA_KV_LORA, MLA_HEADS, MLA_NOPE + MLA_V)
    wkv = jnp.concatenate([wkv[:, :, :MLA_NOPE].reshape(MLA_KV_LORA, -1),
                           wkv[:, :, MLA_NOPE:].reshape(MLA_KV_LORA, -1)], axis=1).astype(_BF16)
    cos_m, sin_m = _rope_tables(pos, MLA_ROPE)
    cos_m = jnp.concatenate([cos_m, cos_m], axis=-1)
    sin_m = jnp.concatenate([sin_m, sin_m], axis=-1)
    q0, k0, v0 = _mla_prep(u0, row(q_a_norm[0]), row(kv_a_norm[0]), wq, wkv, cos_m, sin_m)
    b = None
    for gi, (row0, n_seq, seq) in enumerate(groups):
        b = _flash(q0, k0, v0, b, row0=row0, n_seq=n_seq, seq=seq, heads=MLA_HEADS, kv_group=1,
                   dq=MLA_QK_PAD, dv=MLA_V, q_col0=0, k_col0=0, v_col0=0, tq=_tile(seq, FLASH_TQ),
                   tk=_tile(seq, FLASH_TK), name=f"mla_attn_{gi}")
    x = _out_proj(a, b, w_out0[0].astype(_BF16), x_in)
    x = _ffn(x, row(norm_ffn0[0]), ffn_w_gate[0].astype(_BF16), ffn_w_up[0].astype(_BF16),
             ffn_w_down[0].astype(_BF16))

    lambda_init = 0.8 - 0.6 * math.exp(-0.3 * 1)
    cw = DIFF_HEADS * 2 * DIFF_HD
    col_scale = jnp.concatenate([jnp.full((cw,), DIFF_HD ** -0.5 * LOG2E, _F32),
                                 jnp.ones((w_in1.shape[2] - cw,), _F32)])
    u1 = _rms_matmul((x,), row(norm_mix1[0]), (w_in1[0] * col_scale[None, :]).astype(_BF16), tn=1536)

    c = None
    for gi, (row0, n_seq, seq) in enumerate(groups):
        tq, tk = _tile(seq, DIFF_TQ), _tile(seq, FLASH_TK)
        diff_args = (_bias_tiles(rel_bias, tq, min(tk, FLASH_KH)), row(lambda_q1[0]), row(lambda_k1[0]),
                     row(lambda_q2[0]), row(lambda_k2[0]), row(diff_subln[0]))
        c = _flash(u1, u1, u1, c, row0=row0, n_seq=n_seq, seq=seq, heads=DIFF_HEADS, kv_group=1,
                   dq=2 * DIFF_HD, dv=2 * DIFF_HD, q_col0=0, k_col0=DIFF_HEADS, v_col0=2 * DIFF_HEADS,
                   tq=tq, tk=tk, diff_args=diff_args, lambda_init=lambda_init, name=f"diff_attn_{gi}")

    cr, sr = _rope_tables(pos // GRID_W, AXIAL_DIM)
    cc, sc = _rope_tables(pos % GRID_W, AXIAL_DIM)
    cos_a = jnp.concatenate([cr, cc], axis=-1)
    sin_a = jnp.concatenate([sr, sc], axis=-1)
    qg, kg = _gqa_prep(u1, row(gqa_q_norm[0]), row(gqa_k_norm[0]), cos_a, sin_a)
    v_col0 = (3 * cw + (GQA_HEADS + GQA_KV) * GQA_HD) // GQA_HD
    dd = None
    for gi, (row0, n_seq, seq) in enumerate(groups):
        dd = _flash(qg, kg, u1, dd, row0=row0, n_seq=n_seq, seq=seq, heads=GQA_HEADS, kv_group=GQA_GROUP,
                    dq=GQA_HD, dv=GQA_HD, q_col0=0, k_col0=0, v_col0=v_col0, tq=_tile(seq, FLASH_TQ),
                    tk=_tile(seq, FLASH_TK), name=f"gqa_attn_{gi}")
    x = _out_proj(c, dd, w_out1[0].astype(_BF16), (x,))

    w_r = jnp.pad(router_w[0], ((0, 0), (0, LANES - N_EXPERTS)))
    b_r = jnp.pad(router_b[0], (0, LANES - N_EXPERTS)).reshape(1, LANES)
    h2, route = _router(x, row(norm_ffn1[0]), w_r, b_r)
    experts = route[:, :TOP_K].astype(jnp.int32)
    tm_moe = _tile(n * TOP_K, 1024)
    n_tiles = n * TOP_K // tm_moe + N_EXPERTS + 1
    slot, row_token, tile_expert, n_used = _moe_plan(experts, tm_moe, n_tiles)
    tc = _tile(n, 256)
    pos3 = slot.reshape(n // tc, 1, tc * TOP_K)
    ys = _moe_ffn(h2, row_token.reshape(n_tiles, 1, tm_moe), tile_expert, n_used,
                  moe_w_gate[0].astype(_BF16), moe_w_up[0].astype(_BF16), moe_w_down[0].astype(_BF16))
    y_p, y_s = _combine(pos3, route, x, row(final_norm), ys, n_p)
    return (y_p.reshape(bp, sp, d), y_s.reshape(bs, ss, d))
```

```python
import functools
import math

import jax
import jax.numpy as jnp
from jax import lax
from jax.experimental import pallas as pl
from jax.experimental.pallas import tpu as pltpu

D_MODEL = 2048
RMS_EPS = 1e-6
ROPE_THETA = 10000.0
GRID_W = 64
POOL_WINDOWS = (2, 4, 8, 16)
POOL_GROUP = D_MODEL // 8
POOL_WIDTH = POOL_GROUP * len(POOL_WINDOWS)
MLA_HEADS = 8
MLA_Q_LORA = D_MODEL // 4
MLA_KV_LORA = D_MODEL // 8
MLA_NOPE = 128
MLA_ROPE = 64
MLA_V = 128
MLA_QK_PAD = 256
IN0_WIDTH = POOL_WIDTH + MLA_Q_LORA + MLA_KV_LORA + MLA_ROPE
IN0_PAD = 1920
DIFF_HEADS = 8
DIFF_HD = 64
GQA_HEADS = 8
GQA_KV = 2
GQA_GROUP = GQA_HEADS // GQA_KV
GQA_HD = 128
AXIAL_DIM = GQA_HD // 2
REL_BUCKETS = 32
REL_MAX_DIST = 128
N_EXPERTS = 8
TOP_K = 2
LANES = 128
HALO = 16
FLASH_SUB = 256
FLASH_TQ = 2048
FLASH_TK = 2048
FLASH_KH = 1024
DIFF_TQ = 512
LOG2E = math.log2(math.e)

_F32 = jnp.float32
_BF16 = jnp.bfloat16
_VMEM_LIMIT = 56 * 1024 * 1024
_NEG = -1e30


def _cparams(*sem):
    return pltpu.CompilerParams(dimension_semantics=sem, vmem_limit_bytes=_VMEM_LIMIT)


def _tile(n, pref):
    t = pref
    while n % t:
        t //= 2
    return t


def _rms(x, g):
    ms = jnp.mean(x * x, axis=-1, keepdims=True)
    return x * lax.rsqrt(ms + RMS_EPS) * g


def _rope_partner(x):
    lane = lax.broadcasted_iota(jnp.int32, x.shape, 1)
    first = (lane % 64) < 32
    return jnp.where(first, pltpu.roll(x, 96, 1), pltpu.roll(x, 32, 1))


def _row_parts(parts, tm, col_block, col_of):
    def index_map(first, nt):
        return lambda i, j: (jnp.clip(i - first, 0, nt - 1), col_of(j))

    specs, first = [], 0
    for part in parts:
        assert part.shape[0] % tm == 0
        nt = part.shape[0] // tm
        specs.append(pl.BlockSpec((tm, col_block), index_map(first, nt)))
        first += nt
    return specs


def _select_part(refs, part_tiles):
    val = refs[-1][...]
    end = sum(part_tiles) - part_tiles[-1]
    for ref, nt in zip(reversed(refs[:-1]), reversed(part_tiles[:-1])):
        val = jnp.where(pl.program_id(0) < end, ref[...], val)
        end -= nt
    return val


def _rms_matmul_kernel(*refs, part_tiles):
    x_refs = refs[:len(part_tiles)]
    g_ref, w_ref, o_ref, h_sc = refs[len(part_tiles):]

    @pl.when(pl.program_id(1) == 0)
    def _():
        h_sc[...] = _rms(_select_part(x_refs, part_tiles), g_ref[...]).astype(_BF16)

    o_ref[...] = jnp.dot(h_sc[...], w_ref[...], preferred_element_type=_F32).astype(o_ref.dtype)


def _rms_matmul(x_parts, g, w, tn, tm_pref=1024):
    d = x_parts[0].shape[1]
    n = sum(p.shape[0] for p in x_parts)
    width = w.shape[1]
    tm = _tile(math.gcd(*[p.shape[0] for p in x_parts]), tm_pref)
    part_tiles = tuple(p.shape[0] // tm for p in x_parts)
    return pl.pallas_call(
        functools.partial(_rms_matmul_kernel, part_tiles=part_tiles),
        grid=(n // tm, width // tn),
        in_specs=_row_parts(x_parts, tm, d, lambda j: 0) + [
            pl.BlockSpec((1, d), lambda i, j: (0, 0)),
            pl.BlockSpec((d, tn), lambda i, j: (0, j)),
        ],
        out_specs=pl.BlockSpec((tm, tn), lambda i, j: (i, j)),
        out_shape=jax.ShapeDtypeStruct((n, width), _BF16),
        scratch_shapes=[pltpu.VMEM((tm, d), _BF16)],
        compiler_params=_cparams("parallel", "arbitrary"),
        name="rms_matmul",
    )(*x_parts, g, w)


def _out_proj_kernel(a_ref, b_ref, w_ref, *refs, part_tiles):
    x_refs, o_ref = refs[:-1], refs[-1]
    ka = a_ref.shape[1]
    acc = jnp.dot(a_ref[...], w_ref[:ka, :], preferred_element_type=_F32)
    acc = acc + jnp.dot(b_ref[...], w_ref[ka:, :], preferred_element_type=_F32)
    o_ref[...] = _select_part(x_refs, part_tiles) + acc


def _out_proj(a, b, w, x_parts):
    d = x_parts[0].shape[1]
    n = sum(p.shape[0] for p in x_parts)
    tm = _tile(math.gcd(*[p.shape[0] for p in x_parts]), 512)
    part_tiles = tuple(p.shape[0] // tm for p in x_parts)
    tn = d
    return pl.pallas_call(
        functools.partial(_out_proj_kernel, part_tiles=part_tiles),
        grid=(n // tm, d // tn),
        in_specs=[
            pl.BlockSpec((tm, a.shape[1]), lambda i, j: (i, 0)),
            pl.BlockSpec((tm, b.shape[1]), lambda i, j: (i, 0)),
            pl.BlockSpec((w.shape[0], tn), lambda i, j: (0, j)),
        ] + _row_parts(x_parts, tm, tn, lambda j: j),
        out_specs=pl.BlockSpec((tm, tn), lambda i, j: (i, j)),
        out_shape=jax.ShapeDtypeStruct((n, d), _F32),
        compiler_params=_cparams("parallel", "arbitrary"),
        name="out_proj",
    )(a, b, w, *x_parts)


def _swiglu_acc(h, wg, wu, wd):
    gate = jnp.dot(h, wg, preferred_element_type=_F32)
    up = jnp.dot(h, wu, preferred_element_type=_F32)
    act = (gate / (1.0 + jnp.exp(-gate)) * up).astype(_BF16)
    return jnp.dot(act, wd, preferred_element_type=_F32)


def _ffn_kernel(x_ref, g_ref, wg_ref, wu_ref, wd_ref, o_ref, h_sc):
    @pl.when(pl.program_id(1) == 0)
    def _():
        x = x_ref[...]
        h_sc[...] = _rms(x, g_ref[...]).astype(_BF16)
        o_ref[...] = x

    o_ref[...] += _swiglu_acc(h_sc[...], wg_ref[...], wu_ref[...], wd_ref[...])


def _ffn(x, g, wg, wu, wd):
    n, d = x.shape
    f = wg.shape[1]
    tm = _tile(n, 1024)
    tf = 512
    return pl.pallas_call(
        _ffn_kernel,
        grid=(n // tm, f // tf),
        in_specs=[
            pl.BlockSpec((tm, d), lambda i, j: (i, 0)),
            pl.BlockSpec((1, d), lambda i, j: (0, 0)),
            pl.BlockSpec((d, tf), lambda i, j: (0, j)),
            pl.BlockSpec((d, tf), lambda i, j: (0, j)),
            pl.BlockSpec((tf, d), lambda i, j: (j, 0)),
        ],
        out_specs=pl.BlockSpec((tm, d), lambda i, j: (i, 0)),
        out_shape=jax.ShapeDtypeStruct((n, d), _F32),
        scratch_shapes=[pltpu.VMEM((tm, d), _BF16)],
        compiler_params=_cparams("parallel", "arbitrary"),
        name="ffn_swiglu",
    )(x, g, wg, wu, wd)


def _pool_kernel(pos0_ref, slen_ref, prev_ref, cur_ref, next_ref, w_ref, scale_ref, o_ref, buf_sc):
    i = pl.program_id(0)
    t = cur_ref.shape[0]
    pos0 = pos0_ref[i]
    slen = slen_ref[i]
    cur = cur_ref[...].astype(_F32)
    buf_sc[0:HALO, :] = jnp.where(pos0 > 0, prev_ref[...].astype(_F32), 0.0)
    buf_sc[HALO:HALO + t, :] = cur
    buf_sc[HALO + t:, :] = jnp.where(pos0 + t < slen, next_ref[...].astype(_F32), 0.0)
    tpos = pos0 + lax.broadcasted_iota(jnp.int32, (t, 1), 0)
    for g, win in enumerate(POOL_WINDOWS):
        c0, c1 = g * POOL_GROUP, (g + 1) * POOL_GROUP
        half = win // 2
        acc = buf_sc[HALO - half:HALO - half + t, c0:c1]
        for dlt in range(-half + 1, half):
            acc = acc + buf_sc[HALO + dlt:HALO + dlt + t, c0:c1]
        count = (jnp.minimum(tpos + half, slen) - jnp.maximum(tpos - half, 0)).astype(_F32)
        p = (acc / count - cur[:, c0:c1]).astype(_BF16)
        y = jnp.dot(p, w_ref[g], preferred_element_type=_F32)
        o_ref[:, c0:c1] = (y * scale_ref[:, c0:c1]).astype(o_ref.dtype)


def _pool_mixer(u, pool_w, pool_scale, tile_pos0, tile_slen, t):
    n = u.shape[0]
    hb = t // HALO
    last = n // HALO - 1
    grid_spec = pltpu.PrefetchScalarGridSpec(
        num_scalar_prefetch=2,
        grid=(n // t,),
        in_specs=[
            pl.BlockSpec((HALO, POOL_WIDTH), lambda i, p, s: (jnp.maximum(i * hb - 1, 0), 0)),
            pl.BlockSpec((t, POOL_WIDTH), lambda i, p, s: (i, 0)),
            pl.BlockSpec((HALO, POOL_WIDTH), lambda i, p, s: (jnp.minimum((i + 1) * hb, last), 0)),
            pl.BlockSpec((len(POOL_WINDOWS), POOL_GROUP, POOL_GROUP), lambda i, p, s: (0, 0, 0)),
            pl.BlockSpec((1, POOL_WIDTH), lambda i, p, s: (0, 0)),
        ],
        out_specs=pl.BlockSpec((t, POOL_WIDTH), lambda i, p, s: (i, 0)),
        scratch_shapes=[pltpu.VMEM((t + 2 * HALO, POOL_WIDTH), _F32)],
    )
    return pl.pallas_call(
        _pool_kernel,
        grid_spec=grid_spec,
        out_shape=jax.ShapeDtypeStruct((n, POOL_WIDTH), _BF16),
        compiler_params=_cparams("arbitrary"),
        name="pool_mixer",
    )(tile_pos0, tile_slen, u, u, u, pool_w, pool_scale)


def _mla_prep_kernel(ql_ref, kvl_ref, kr_ref, gq_ref, gkv_ref, wq_ref, wkv_ref, cos_ref, sin_ref,
                     q_ref, k_ref, v_ref):
    scale = (MLA_NOPE + MLA_ROPE) ** -0.5 * LOG2E
    cos = cos_ref[...]
    sin = sin_ref[...]
    lane = lax.broadcasted_iota(jnp.int32, cos.shape, 1)
    low = lane < MLA_ROPE

    qn = _rms(ql_ref[...].astype(_F32), gq_ref[...]).astype(_BF16)
    q = jnp.dot(qn, wq_ref[...], preferred_element_type=_F32)
    kvn = _rms(kvl_ref[...].astype(_F32), gkv_ref[...]).astype(_BF16)
    kv = jnp.dot(kvn, wkv_ref[...], preferred_element_type=_F32)
    nope_w = MLA_HEADS * MLA_NOPE
    v_ref[...] = kv[:, nope_w:].astype(v_ref.dtype)

    kr = kr_ref[...].astype(_F32)
    k_pe = jnp.where(low, kr * cos + _rope_partner(kr) * sin, 0.0).astype(k_ref.dtype)
    for pair in range(MLA_HEADS // 2):
        pe = q[:, nope_w + pair * LANES:nope_w + (pair + 1) * LANES]
        roped = (pe * cos + _rope_partner(pe) * sin) * scale
        for sub in range(2):
            h = 2 * pair + sub
            pe_h = roped if sub == 0 else pltpu.roll(roped, MLA_ROPE, 1)
            base = h * MLA_QK_PAD
            q_ref[:, base:base + MLA_NOPE] = (q[:, h * MLA_NOPE:(h + 1) * MLA_NOPE] * scale).astype(q_ref.dtype)
            q_ref[:, base + MLA_NOPE:base + MLA_QK_PAD] = jnp.where(low, pe_h, 0.0).astype(q_ref.dtype)
            k_ref[:, base:base + MLA_NOPE] = kv[:, h * MLA_NOPE:(h + 1) * MLA_NOPE].astype(k_ref.dtype)
            k_ref[:, base + MLA_NOPE:base + MLA_QK_PAD] = k_pe


def _mla_prep(u, gq, gkv, wq, wkv, cos_t, sin_t):
    n = u.shape[0]
    tm = _tile(n, 512)
    qk_w = MLA_HEADS * MLA_QK_PAD
    v_w = MLA_HEADS * MLA_V
    return pl.pallas_call(
        _mla_prep_kernel,
        grid=(n // tm,),
        in_specs=[
            pl.BlockSpec((tm, MLA_Q_LORA), lambda i: (i, POOL_WIDTH // MLA_Q_LORA)),
            pl.BlockSpec((tm, MLA_KV_LORA), lambda i: (i, (POOL_WIDTH + MLA_Q_LORA) // MLA_KV_LORA)),
            pl.BlockSpec((tm, LANES), lambda i: (i, (POOL_WIDTH + MLA_Q_LORA + MLA_KV_LORA) // LANES)),
            pl.BlockSpec((1, MLA_Q_LORA), lambda i: (0, 0)),
            pl.BlockSpec((1, MLA_KV_LORA), lambda i: (0, 0)),
            pl.BlockSpec(wq.shape, lambda i: (0, 0)),
            pl.BlockSpec(wkv.shape, lambda i: (0, 0)),
            pl.BlockSpec((tm, LANES), lambda i: (i, 0)),
            pl.BlockSpec((tm, LANES), lambda i: (i, 0)),
        ],
        out_specs=[
            pl.BlockSpec((tm, qk_w), lambda i: (i, 0)),
            pl.BlockSpec((tm, qk_w), lambda i: (i, 0)),
            pl.BlockSpec((tm, v_w), lambda i: (i, 0)),
        ],
        out_shape=[
            jax.ShapeDtypeStruct((n, qk_w), _BF16),
            jax.ShapeDtypeStruct((n, qk_w), _BF16),
            jax.ShapeDtypeStruct((n, v_w), _BF16),
        ],
        compiler_params=_cparams("parallel"),
        name="mla_prep",
    )(u, u, u, gq, gkv, wq, wkv, cos_t, sin_t)


def _gqa_prep_kernel(dq_ref, dk_ref, gq_ref, gk_ref, cos_ref, sin_ref, q_ref, k_ref):
    scale = GQA_HD ** -0.5 * LOG2E
    cos = cos_ref[...]
    sin = sin_ref[...]

    def norm_rope(x, g):
        xn = _rms(x.astype(_F32), g)
        return xn * cos + _rope_partner(xn) * sin

    for h in range(GQA_HEADS):
        sl = slice(h * GQA_HD, (h + 1) * GQA_HD)
        q_ref[:, sl] = (norm_rope(dq_ref[:, sl], gq_ref[...]) * scale).astype(q_ref.dtype)
    for h in range(GQA_KV):
        sl = slice(h * GQA_HD, (h + 1) * GQA_HD)
        k_ref[:, sl] = norm_rope(dk_ref[:, sl], gk_ref[...]).astype(k_ref.dtype)


def _gqa_prep(u, gq, gk, cos_t, sin_t):
    n = u.shape[0]
    tm = _tile(n, 512)
    q_w = GQA_HEADS * GQA_HD
    k_w = GQA_KV * GQA_HD
    dq_off = 3 * DIFF_HEADS * 2 * DIFF_HD
    return pl.pallas_call(
        _gqa_prep_kernel,
        grid=(n // tm,),
        in_specs=[
            pl.BlockSpec((tm, q_w), lambda i: (i, dq_off // q_w)),
            pl.BlockSpec((tm, k_w), lambda i: (i, (dq_off + q_w) // k_w)),
            pl.BlockSpec((1, GQA_HD), lambda i: (0, 0)),
            pl.BlockSpec((1, GQA_HD), lambda i: (0, 0)),
            pl.BlockSpec((tm, LANES), lambda i: (i, 0)),
            pl.BlockSpec((tm, LANES), lambda i: (i, 0)),
        ],
        out_specs=[
            pl.BlockSpec((tm, q_w), lambda i: (i, 0)),
            pl.BlockSpec((tm, k_w), lambda i: (i, 0)),
        ],
        out_shape=[
            jax.ShapeDtypeStruct((n, q_w), _BF16),
            jax.ShapeDtypeStruct((n, k_w), _BF16),
        ],
        compiler_params=_cparams("parallel"),
        name="gqa_prep",
    )(u, u, gq, gk, cos_t, sin_t)


def _flash_step(q_src, k_ref, v_ref, bias_refs, m_sc, l_sc, acc_sc, tq):
    tk = k_ref.shape[0]
    kh = min(tk, FLASH_KH)
    n_chunks = kh // LANES
    starts = range(0, m_sc.shape[0], FLASH_SUB)

    def score(half, r0):
        k = k_ref[half * kh:(half + 1) * kh, :]
        return lax.dot_general(q_src[r0:r0 + FLASH_SUB, :], k, (((1,), (1,)), ((), ())),
                               preferred_element_type=_F32)

    def softmax_pv(half, r0, s):
        sl = slice(r0, r0 + FLASH_SUB)
        chunks = [s[:, c * LANES:(c + 1) * LANES] for c in range(n_chunks)]
        if bias_refs is not None:
            b0 = r0 % tq
            chunks = [bias_refs[half][b0:b0 + FLASH_SUB, c * LANES:(c + 1) * LANES] + ch
                      for c, ch in enumerate(chunks)]
        m_prev = m_sc[sl, :]
        row_max = jnp.max(functools.reduce(jnp.maximum, chunks), axis=-1, keepdims=True)
        m_new = jnp.maximum(m_prev, row_max)
        alpha = jnp.exp2(m_prev - m_new)
        ps = [jnp.exp2(ch - m_new) for ch in chunks]
        l_sc[sl, :] = alpha * l_sc[sl, :] + functools.reduce(jnp.add, ps)
        p = jnp.concatenate(ps, axis=1).astype(_BF16)
        v = v_ref[half * kh:(half + 1) * kh, :]
        acc_sc[sl, :] = alpha * acc_sc[sl, :] + jnp.dot(p, v, preferred_element_type=_F32)
        m_sc[sl, :] = m_new

    scores = [score(0, r0) for r0 in starts]
    for half in range(tk // kh):
        nxt = []
        for r0, s in zip(starts, scores):
            softmax_pv(half, r0, s)
            if (half + 1) * kh < tk:
                nxt.append(score(half + 1, r0))
        scores = nxt


def _flash_kernel(*refs, diff, n_halves, lambda_init):
    if diff:
        q_ref, k_ref, v_ref = refs[:3]
        bias_refs = refs[3:3 + n_halves]
        (lq1_ref, lk1_ref, lq2_ref, lk2_ref, sub_ref, _, o_ref,
         q_sc, m_sc, l_sc, acc_sc) = refs[3 + n_halves:]
    else:
        q_ref, k_ref, v_ref, _, o_ref, m_sc, l_sc, acc_sc = refs
    ki = pl.program_id(3)
    tq = q_ref.shape[0]

    @pl.when(ki == 0)
    def _():
        m_sc[...] = jnp.full_like(m_sc, _NEG)
        l_sc[...] = jnp.zeros_like(l_sc)
        acc_sc[...] = jnp.zeros_like(acc_sc)
        if diff:
            q = q_ref[...]
            lane = lax.broadcasted_iota(jnp.int32, q.shape, 1)
            zero = jnp.zeros_like(q)
            q_sc[0:tq, :] = jnp.where(lane < DIFF_HD, q, zero)
            q_sc[tq:, :] = jnp.where(lane >= DIFF_HD, q, zero)

    if diff:
        _flash_step(q_sc, k_ref, v_ref, bias_refs, m_sc, l_sc, acc_sc, tq)
    else:
        _flash_step(q_ref, k_ref, v_ref, None, m_sc, l_sc, acc_sc, tq)

    @pl.when(ki == pl.num_programs(3) - 1)
    def _():
        o = acc_sc[...] / jnp.sum(l_sc[...], axis=-1, keepdims=True)
        if diff:
            lam = (jnp.exp(jnp.sum(lq1_ref[...] * lk1_ref[...], axis=-1, keepdims=True))
                   - jnp.exp(jnp.sum(lq2_ref[...] * lk2_ref[...], axis=-1, keepdims=True))
                   + lambda_init)
            o = o[0:tq, :] - lam * o[tq:, :]
            o = _rms(o, sub_ref[...]) * (1.0 - lambda_init)
        o_ref[...] = o.astype(o_ref.dtype)


def _flash(q_arr, k_arr, v_arr, prev_out, *, row0, n_seq, seq, heads, kv_group, dq, dv,
           q_col0, k_col0, v_col0, tq, tk, diff_args=None, lambda_init=0.0, name):
    n = q_arr.shape[0]
    nq, nk = seq // tq, seq // tk
    diff = diff_args is not None
    rows = 2 * tq if diff else tq
    assert rows % FLASH_SUB == 0 and tq % FLASH_SUB == 0

    def q_map(b, h, qi, ki):
        return (row0 // tq + b * nq + qi, q_col0 + h)

    def k_map(b, h, qi, ki):
        return (row0 // tk + b * nk + ki, k_col0 + h // kv_group)

    def v_map(b, h, qi, ki):
        return (row0 // tk + b * nk + ki, v_col0 + h // kv_group)

    def o_map(b, h, qi, ki):
        return (row0 // tq + b * nq + qi, h)

    in_specs = [pl.BlockSpec((tq, dq), q_map), pl.BlockSpec((tk, dq), k_map), pl.BlockSpec((tk, dv), v_map)]
    args = [q_arr, k_arr, v_arr]
    scratch = []
    kh = min(tk, FLASH_KH)
    n_halves = tk // kh
    if diff:
        assert kh % tq == 0 and tq >= REL_MAX_DIST
        ratio = kh // tq
        bias, lq1, lk1, lq2, lk2, sub = diff_args
        assert bias.shape[1:] == (ratio + 4, tq, kh)

        def bias_map(half):
            def index_map(b, h, qi, ki):
                off = ratio * (n_halves * ki + half) - qi
                return (h, jnp.clip(off, -ratio - 1, 2) + ratio + 1, 0, 0)
            return index_map

        for half in range(n_halves):
            in_specs.append(pl.BlockSpec((None, None, tq, kh), bias_map(half)))
            args.append(bias)
        in_specs += [pl.BlockSpec((1, DIFF_HD), lambda b, h, qi, ki: (0, 0))] * 4
        in_specs.append(pl.BlockSpec((1, 2 * DIFF_HD), lambda b, h, qi, ki: (0, 0)))
        args += [lq1, lk1, lq2, lk2, sub]
        scratch.append(pltpu.VMEM((rows, dq), _BF16))
    out_shape = jax.ShapeDtypeStruct((n, heads * dv), _BF16)
    if prev_out is None:
        prev_out = jnp.zeros(out_shape.shape, out_shape.dtype)
    aliases = {len(args): 0}
    in_specs.append(pl.BlockSpec(memory_space=pl.ANY))
    args.append(prev_out)
    assert dv == LANES
    scratch += [pltpu.VMEM((rows, LANES), _F32), pltpu.VMEM((rows, LANES), _F32), pltpu.VMEM((rows, dv), _F32)]
    return pl.pallas_call(
        functools.partial(_flash_kernel, diff=diff, n_halves=n_halves, lambda_init=lambda_init),
        grid=(n_seq, heads, nq, nk),
        in_specs=in_specs,
        out_specs=pl.BlockSpec((tq, dv), o_map),
        out_shape=out_shape,
        scratch_shapes=scratch,
        input_output_aliases=aliases,
        compiler_params=_cparams("parallel", "parallel", "parallel", "arbitrary"),
        name=name,
    )(*args)


def _router_kernel(x_ref, g_ref, w_ref, b_ref, h_ref, route_ref):
    h = _rms(x_ref[...], g_ref[...])
    h_ref[...] = h
    logits = jnp.dot(h, w_ref[...], preferred_element_type=_F32, precision=lax.Precision.HIGHEST) + b_ref[...]
    lane = lax.broadcasted_iota(jnp.int32, logits.shape, 1)
    logits = jnp.where(lane < N_EXPERTS, logits, _NEG)
    v1 = jnp.max(logits, axis=-1, keepdims=True)
    i1 = jnp.min(jnp.where(logits == v1, lane, LANES), axis=-1, keepdims=True)
    rest = jnp.where(lane == i1, _NEG, logits)
    v2 = jnp.max(rest, axis=-1, keepdims=True)
    i2 = jnp.min(jnp.where(rest == v2, lane, LANES), axis=-1, keepdims=True)
    g1 = 1.0 / (1.0 + jnp.exp(v2 - v1))
    g2 = 1.0 - g1
    route = jnp.where(lane == 0, i1.astype(_F32),
                      jnp.where(lane == 1, i2.astype(_F32),
                                jnp.where(lane == 2, g1, jnp.where(lane == 3, g2, 0.0))))
    route_ref[...] = route


def _router(x, g, w_pad, b_pad):
    n, d = x.shape
    tm = _tile(n, 512)
    return pl.pallas_call(
        _router_kernel,
        grid=(n // tm,),
        in_specs=[
            pl.BlockSpec((tm, d), lambda i: (i, 0)),
            pl.BlockSpec((1, d), lambda i: (0, 0)),
            pl.BlockSpec((d, LANES), lambda i: (0, 0)),
            pl.BlockSpec((1, LANES), lambda i: (0, 0)),
        ],
        out_specs=[pl.BlockSpec((tm, d), lambda i: (i, 0)), pl.BlockSpec((tm, LANES), lambda i: (i, 0))],
        out_shape=[jax.ShapeDtypeStruct((n, d), _F32), jax.ShapeDtypeStruct((n, LANES), _F32)],
        compiler_params=_cparams("parallel"),
        name="moe_router",
    )(x, g, w_pad, b_pad)


def _gather_rows(src_hbm, idx_ref, idx0, stride, dst, n_rows, last, sem, *, wait):
    def step(i, carry):
        row = idx_ref[0, 0, idx0 + stride * jnp.minimum(i, last)]
        copy = pltpu.make_async_copy(src_hbm.at[pl.ds(row, 1), :], dst.at[pl.ds(i, 1), :], sem)
        if wait:
            copy.wait()
        else:
            copy.start()
        return carry

    lax.fori_loop(0, n_rows, step, 0, unroll=8)


def _moe_ffn_kernel(te_ref, nused_ref, tok_ref, next_tok_ref, h_ref, wg_ref, wu_ref, wd_ref, ys_ref,
                    xf_sc, xb_sc, sem):
    r = pl.program_id(0)
    j = pl.program_id(1)
    n_used = nused_ref[0]
    tm = xb_sc.shape[0]
    n_stage = xf_sc.shape[0]
    per_step = n_stage // pl.num_programs(1)

    @pl.when(j == 0)
    def _():
        ys_ref[...] = jnp.zeros_like(ys_ref)

        @pl.when(r == 0)
        def _():
            _gather_rows(h_ref, tok_ref, 0, 1, xf_sc, n_stage, tm - 1, sem, wait=False)

        @pl.when(r <= n_used)
        def _():
            _gather_rows(h_ref, tok_ref, 0, 1, xf_sc, n_stage, tm - 1, sem, wait=True)
            xb_sc[...] = xf_sc[0:tm, :].astype(_BF16)

    @pl.when(r < n_used)
    def _():
        for t in range(per_step):
            i = j * per_step + t
            row = next_tok_ref[0, 0, jnp.minimum(i, tm - 1)]
            pltpu.make_async_copy(h_ref.at[pl.ds(row, 1), :], xf_sc.at[pl.ds(i, 1), :], sem).start()
        ys_ref[...] += _swiglu_acc(xb_sc[...], wg_ref[...], wu_ref[...], wd_ref[...])


def _moe_ffn(h, row_token3, tile_expert, n_used, wg, wu, wd):
    n_tiles, _, tm = row_token3.shape
    d = h.shape[1]
    f = wg.shape[2]
    tf = 512
    n_steps = f // tf
    n_stage = -(-tm // n_steps) * n_steps
    grid_spec = pltpu.PrefetchScalarGridSpec(
        num_scalar_prefetch=2,
        grid=(n_tiles, n_steps),
        in_specs=[
            pl.BlockSpec((1, 1, tm), lambda r, j, te, nu: (r, 0, 0), memory_space=pltpu.SMEM),
            pl.BlockSpec((1, 1, tm), lambda r, j, te, nu: (jnp.minimum(r + 1, n_tiles - 1), 0, 0),
                         memory_space=pltpu.SMEM),
            pl.BlockSpec(memory_space=pl.ANY),
            pl.BlockSpec((None, d, tf), lambda r, j, te, nu: (te[r], 0, j)),
            pl.BlockSpec((None, d, tf), lambda r, j, te, nu: (te[r], 0, j)),
            pl.BlockSpec((None, tf, d), lambda r, j, te, nu: (te[r], j, 0)),
        ],
        out_specs=pl.BlockSpec((tm, d), lambda r, j, te, nu: (r, 0)),
        scratch_shapes=[pltpu.VMEM((n_stage, d), _F32), pltpu.VMEM((tm, d), _BF16),
                        pltpu.SemaphoreType.DMA(())],
    )
    return pl.pallas_call(
        _moe_ffn_kernel,
        grid_spec=grid_spec,
        out_shape=jax.ShapeDtypeStruct((n_tiles * tm, d), _F32),
        compiler_params=_cparams("arbitrary", "arbitrary"),
        name="moe_expert_ffn",
    )(tile_expert, n_used, row_token3, row_token3, h, wg, wu, wd)


def _combine_kernel(pos_ref, route_ref, x_ref, g_ref, ys_ref, o_first_ref, o_rest_ref, buf, sem, *, n_first):
    tc = x_ref.shape[0]
    for wait in (False, True):
        for k in range(TOP_K):
            _gather_rows(ys_ref, pos_ref, k, TOP_K, buf.at[k], tc, tc - 1, sem.at[k], wait=wait)
    route = route_ref[...]
    y = x_ref[...] + route[:, 2:3] * buf[0] + route[:, 3:4] * buf[1]
    out = _rms(y, g_ref[...])

    @pl.when(pl.program_id(0) < n_first)
    def _():
        o_first_ref[...] = out

    @pl.when(pl.program_id(0) >= n_first)
    def _():
        o_rest_ref[...] = out


def _combine(pos3, route, x, g, ys, n_first_rows):
    n, d = x.shape
    n_chunks, _, width = pos3.shape
    tc = width // TOP_K
    assert n_first_rows % tc == 0 and 0 < n_first_rows < n
    n_first = n_first_rows // tc
    return pl.pallas_call(
        functools.partial(_combine_kernel, n_first=n_first),
        grid=(n_chunks,),
        in_specs=[
            pl.BlockSpec((1, 1, width), lambda c: (c, 0, 0), memory_space=pltpu.SMEM),
            pl.BlockSpec((tc, LANES), lambda c: (c, 0)),
            pl.BlockSpec((tc, d), lambda c: (c, 0)),
            pl.BlockSpec((1, d), lambda c: (0, 0)),
            pl.BlockSpec(memory_space=pl.ANY),
        ],
        out_specs=[pl.BlockSpec((tc, d), lambda c: (jnp.minimum(c, n_first - 1), 0)),
                   pl.BlockSpec((tc, d), lambda c: (jnp.maximum(c - n_first, 0), 0))],
        out_shape=[jax.ShapeDtypeStruct((n_first_rows, d), _F32),
                   jax.ShapeDtypeStruct((n - n_first_rows, d), _F32)],
        scratch_shapes=[pltpu.VMEM((TOP_K, tc, d), _F32), pltpu.SemaphoreType.DMA((TOP_K,))],
        compiler_params=_cparams("arbitrary"),
        name="moe_combine",
    )(pos3, route, x, g, ys)


def _rope_tables(pos, dim):
    inv = ROPE_THETA ** (-jnp.arange(0, dim, 2, dtype=_F32) / dim)
    ang = pos.astype(_F32)[:, None] * inv[None, :]
    c, s = jnp.cos(ang), jnp.sin(ang)
    return jnp.concatenate([c, c], axis=-1), jnp.concatenate([-s, s], axis=-1)


def _t5_bucket(rel):
    nb = REL_BUCKETS // 2
    max_exact = nb // 2
    ret = jnp.where(rel > 0, nb, 0)
    n = jnp.abs(rel)
    n_f = jnp.maximum(n, 1).astype(_F32)
    large = max_exact + (jnp.log(n_f / max_exact) / math.log(REL_MAX_DIST / max_exact)
                         * (nb - max_exact)).astype(jnp.int32)
    large = jnp.minimum(large, nb - 1)
    return ret + jnp.where(n < max_exact, n, large)


def _bias_tiles(rel_bias, tq, tk):
    ratio = tk // tq
    table = rel_bias.astype(_F32) * LOG2E
    off = jnp.arange(-ratio, 2)[:, None, None]
    rel = off * tq + jnp.arange(tk)[None, None, :] - jnp.arange(tq)[None, :, None]
    onehot = (_t5_bucket(rel)[..., None] == jnp.arange(REL_BUCKETS)).astype(_F32)
    near = jnp.einsum("dqkb,bh->hdqk", onehot, table, precision=lax.Precision.HIGHEST)
    far = table[_t5_bucket(jnp.array([-REL_MAX_DIST, REL_MAX_DIST]))].T
    far = jnp.broadcast_to(far[:, :, None, None], (far.shape[0], 2, tq, tk))
    return jnp.concatenate([far[:, :1], near, far[:, 1:]], axis=1)


def _moe_plan(experts, tm, n_tiles):
    e_flat = experts.reshape(-1)
    onehot = (e_flat[:, None] == jnp.arange(N_EXPERTS)[None, :]).astype(jnp.int32)
    csum = jnp.cumsum(onehot, axis=0)
    counts = csum[-1]
    rank = jnp.sum(onehot * (csum - 1), axis=1)
    padded = ((counts + tm - 1) // tm) * tm
    ends = jnp.cumsum(padded)
    starts = ends - padded
    pos = starts[e_flat] + rank
    tile_start = jnp.arange(n_tiles, dtype=jnp.int32) * tm
    tile_expert = jnp.minimum(jnp.sum(tile_start[:, None] >= ends[None, :], axis=1), N_EXPERTS - 1)
    n_used = (ends[-1] // tm).reshape(1)
    pair_token = jnp.arange(e_flat.shape[0], dtype=jnp.int32) // TOP_K
    row_token = jnp.zeros((n_tiles * tm,), jnp.int32).at[pos].set(pair_token, unique_indices=True)
    return pos.astype(jnp.int32), row_token, tile_expert.astype(jnp.int32), n_used.astype(jnp.int32)


def kernel(x_prompt, x_sample, rel_bias, norm_mix0, w_in0, pool_w, pool_scale, q_a_norm, w_q_up, kv_a_norm, w_kv_up, w_out0, norm_ffn0, ffn_w_gate, ffn_w_up, ffn_w_down, norm_mix1, w_in1, lambda_q1, lambda_k1, lambda_q2, lambda_k2, diff_subln, gqa_q_norm, gqa_k_norm, w_out1, norm_ffn1, router_w, router_b, moe_w_gate, moe_w_up, moe_w_down, final_norm):
    d = D_MODEL
    bp, sp, _ = x_prompt.shape
    bs, ss, _ = x_sample.shape
    n_p, n_s = bp * sp, bs * ss
    n = n_p + n_s
    groups = [(0, bp, sp), (n_p, bs, ss)]
    x_in = (x_prompt.reshape(n_p, d), x_sample.reshape(n_s, d))
    pos = jnp.concatenate([jnp.tile(jnp.arange(sp), bp), jnp.tile(jnp.arange(ss), bs)])
    row = lambda v: v.reshape(1, -1).astype(_F32)

    w_in0_p = jnp.pad(w_in0[0], ((0, 0), (0, IN0_PAD - IN0_WIDTH))).astype(_BF16)
    u0 = _rms_matmul(x_in, row(norm_mix0[0]), w_in0_p, tn=IN0_PAD, tm_pref=512)

    t_pool = _tile(math.gcd(sp, ss), 512)
    tile_rows = jnp.arange(n // t_pool) * t_pool
    tile_pos0 = pos[tile_rows].astype(jnp.int32)
    tile_slen = jnp.where(tile_rows < n_p, sp, ss).astype(jnp.int32)
    a = _pool_mixer(u0, pool_w[0].astype(_BF16), row(pool_scale[0]), tile_pos0, tile_slen, t_pool)

    wq = w_q_up[0].reshape(MLA_Q_LORA, MLA_HEADS, MLA_NOPE + MLA_ROPE)
    wq = jnp.concatenate([wq[:, :, :MLA_NOPE].reshape(MLA_Q_LORA, -1),
                          wq[:, :, MLA_NOPE:].reshape(MLA_Q_LORA, -1)], axis=1).astype(_BF16)
    wkv = w_kv_up[0].reshape(MLA_KV_LORA, MLA_HEADS, MLA_NOPE + MLA_V)
    wkv = jnp.concatenate([wkv[:, :, :MLA_NOPE].reshape(MLA_KV_LORA, -1),
                           wkv[:, :, MLA_NOPE:].reshape(MLA_KV_LORA, -1)], axis=1).astype(_BF16)
    cos_m, sin_m = _rope_tables(pos, MLA_ROPE)
    cos_m = jnp.concatenate([cos_m, cos_m], axis=-1)
    sin_m = jnp.concatenate([sin_m, sin_m], axis=-1)
    q0, k0, v0 = _mla_prep(u0, row(q_a_norm[0]), row(kv_a_norm[0]), wq, wkv, cos_m, sin_m)
    b = None
    for gi, (row0, n_seq, seq) in enumerate(groups):
        b = _flash(q0, k0, v0, b, row0=row0, n_seq=n_seq, seq=seq, heads=MLA_HEADS, kv_group=1,
                   dq=MLA_QK_PAD, dv=MLA_V, q_col0=0, k_col0=0, v_col0=0, tq=_tile(seq, FLASH_TQ),
                   tk=_tile(seq, FLASH_TK), name=f"mla_attn_{gi}")
    x = _out_proj(a, b, w_out0[0].astype(_BF16), x_in)
    x = _ffn(x, row(norm_ffn0[0]), ffn_w_gate[0].astype(_BF16), ffn_w_up[0].astype(_BF16),
             ffn_w_down[0].astype(_BF16))

    lambda_init = 0.8 - 0.6 * math.exp(-0.3 * 1)
    cw = DIFF_HEADS * 2 * DIFF_HD
    col_scale = jnp.concatenate([jnp.full((cw,), DIFF_HD ** -0.5 * LOG2E, _F32),
                                 jnp.ones((w_in1.shape[2] - cw,), _F32)])
    u1 = _rms_matmul((x,), row(norm_mix1[0]), (w_in1[0] * col_scale[None, :]).astype(_BF16), tn=1536)

    c = None
    for gi, (row0, n_seq, seq) in enumerate(groups):
        tq, tk = _tile(seq, DIFF_TQ), _tile(seq, FLASH_TK)
        diff_args = (_bias_tiles(rel_bias, tq, min(tk, FLASH_KH)), row(lambda_q1[0]), row(lambda_k1[0]),
                     row(lambda_q2[0]), row(lambda_k2[0]), row(diff_subln[0]))
        c = _flash(u1, u1, u1, c, row0=row0, n_seq=n_seq, seq=seq, heads=DIFF_HEADS, kv_group=1,
                   dq=2 * DIFF_HD, dv=2 * DIFF_HD, q_col0=0, k_col0=DIFF_HEADS, v_col0=2 * DIFF_HEADS,
                   tq=tq, tk=tk, diff_args=diff_args, lambda_init=lambda_init, name=f"diff_attn_{gi}")

    cr, sr = _rope_tables(pos // GRID_W, AXIAL_DIM)
    cc, sc = _rope_tables(pos % GRID_W, AXIAL_DIM)
    cos_a = jnp.concatenate([cr, cc], axis=-1)
    sin_a = jnp.concatenate([sr, sc], axis=-1)
    qg, kg = _gqa_prep(u1, row(gqa_q_norm[0]), row(gqa_k_norm[0]), cos_a, sin_a)
    v_col0 = (3 * cw + (GQA_HEADS + GQA_KV) * GQA_HD) // GQA_HD
    dd = None
    for gi, (row0, n_seq, seq) in enumerate(groups):
        dd = _flash(qg, kg, u1, dd, row0=row0, n_seq=n_seq, seq=seq, heads=GQA_HEADS, kv_group=GQA_GROUP,
                    dq=GQA_HD, dv=GQA_HD, q_col0=0, k_col0=0, v_col0=v_col0, tq=_tile(seq, FLASH_TQ),
                    tk=_tile(seq, FLASH_TK), name=f"gqa_attn_{gi}")
    x = _out_proj(c, dd, w_out1[0].astype(_BF16), (x,))

    w_r = jnp.pad(router_w[0], ((0, 0), (0, LANES - N_EXPERTS)))
    b_r = jnp.pad(router_b[0], (0, LANES - N_EXPERTS)).reshape(1, LANES)
    h2, route = _router(x, row(norm_ffn1[0]), w_r, b_r)
    experts = route[:, :TOP_K].astype(jnp.int32)
    tm_moe = _tile(n * TOP_K, 1024)
    n_tiles = n * TOP_K // tm_moe + N_EXPERTS + 1
    slot, row_token, tile_expert, n_used = _moe_plan(experts, tm_moe, n_tiles)
    tc = _tile(n, 512)
    pos3 = slot.reshape(n // tc, 1, tc * TOP_K)
    ys = _moe_ffn(h2, row_token.reshape(n_tiles, 1, tm_moe), tile_expert, n_used,
                  moe_w_gate[0].astype(_BF16), moe_w_up[0].astype(_BF16), moe_w_down[0].astype(_BF16))
    y_p, y_s = _combine(pos3, route, x, row(final_norm), ys, n_p)
    return (y_p.reshape(bp, sp, d), y_s.reshape(bs, ss, d))
```
